```python
import math
import jax, jax.numpy as jnp
from jax import lax
import numpy as np

D_MODEL = 2048
BATCH = 2
SEQ = 8192
DEPTH = 1
DEC_BATCH = 32
DEC_SEQ = 8
PAST_LEN = 16384
PAGE_SIZE = 128

MIX_WIDTH = D_MODEL
NSA_WIDTH = MIX_WIDTH // 2
HG_WIDTH = MIX_WIDTH - NSA_WIDTH
HEAD_DIM = 128
NSA_HEADS = NSA_WIDTH // HEAD_DIM
NSA_KV = 2
NSA_GROUP = NSA_HEADS // NSA_KV
KV_COLS = 2 * NSA_KV * HEAD_DIM
CMP_BLOCK = 32
CMP_STRIDE = 16
CMP_HID = HEAD_DIM
SEL_BLOCK = 64
SEL_RATIO = SEL_BLOCK // CMP_STRIDE
N_SEL = 16
SEL_SPAN_W = (1.0, 2.0, 2.0, 2.0, 1.0)
WINDOW = 512
Q_BLOCK = 128
HG_HEAD_DIM = 128
HG_HEADS = HG_WIDTH // HG_HEAD_DIM
HG_CHUNK = 64
D_FF = (8 * D_MODEL // 3 + 127) // 128 * 128
N_IN = NSA_WIDTH + 3 * KV_COLS + 3 * NSA_HEADS + 4 * HG_WIDTH
ATTN_SCALE = HEAD_DIM ** -0.5
HG_SCALE = HG_HEAD_DIM ** -0.5
EPS = 1e-6

kernel_name = 'hymba_nsa_hgrn2_macaron_step'


def rmsnorm(x, g):
    xf = x.astype(jnp.float32)
    y = xf * lax.rsqrt(jnp.mean(xf * xf, axis=-1, keepdims=True) + EPS)
    return (y * g.astype(jnp.float32)).astype(x.dtype)


def swiglu_half_step(x, g, w_gate, w_up, w_down):
    h = rmsnorm(x, g)
    return x + 0.5 * ((jax.nn.silu(h @ w_gate) * (h @ w_up)) @ w_down)


def masked_softmax(s, mask):
    s = jnp.where(mask, s.astype(jnp.float32), -jnp.inf)
    m = jnp.max(s, axis=-1, keepdims=True)
    e = jnp.exp(s - jnp.where(jnp.isfinite(m), m, 0.0))
    return e / jnp.maximum(jnp.sum(e, axis=-1, keepdims=True), 1e-30)


def split_mix(z):
    sizes = (NSA_WIDTH, KV_COLS, KV_COLS, KV_COLS, 3 * NSA_HEADS, HG_WIDTH, HG_WIDTH, HG_WIDTH, HG_WIDTH)
    points = np.cumsum(sizes)[:-1].tolist()
    q, kvc, kvs, kvw, gt, hq, hf, hi, hg = jnp.split(z, points, axis=-1)
    lead = z.shape[:-1]
    q = q.reshape(*lead, NSA_KV, NSA_GROUP, HEAD_DIM)
    def kv(a):
        return a.reshape(*lead, 2, NSA_KV, HEAD_DIM)
    gates = jax.nn.sigmoid(gt).reshape(*lead, 3, NSA_KV, NSA_GROUP)
    return q, kv(kvc), kv(kvs), kv(kvw), gates, hq, hf, hi, hg


def cmp_half_proj(rows, w1):
    b, l = rows.shape[:2]
    r = rows.reshape(b, l // CMP_STRIDE, CMP_STRIDE, 2, NSA_KV, HEAD_DIM)
    w = w1.reshape(2, CMP_BLOCK // CMP_STRIDE, CMP_STRIDE, HEAD_DIM, CMP_HID)
    return jnp.einsum('bnrckd,cprdh->bnpckh', r, w)


def cmp_finish(hp, pe, w1, b1, w2):
    nh, r = hp.shape[1], hp.shape[2]
    nc = nh - r + 1
    bias = jnp.einsum('crd,crdh->ch', pe, w1) + b1
    pre = sum(hp[:, p:p + nc, p] for p in range(r)) + bias[:, None, :]
    out = jnp.einsum('bnckh,chd->bnckd', jax.nn.silu(pre), w2)
    return out[:, :, 0], out[:, :, 1]


def cmp_to_sel(imp, ns):
    nc = imp.shape[-1]
    pad = [(0, 0)] * (imp.ndim - 1) + [(1, SEL_RATIO * ns + SEL_RATIO - 1 - nc)]
    padded = jnp.pad(imp, pad)
    return sum(w * padded[..., d:d + SEL_RATIO * ns:SEL_RATIO] for d, w in enumerate(SEL_SPAN_W))


def sel_attn(q, sk, sv, idx, tq):
    t, kv, nsel = idx.shape
    spos = idx[..., None] * SEL_BLOCK + jnp.arange(SEL_BLOCK)
    mask = (spos <= tq[:, None, None, None]).reshape(t, kv, 1, nsel * SEL_BLOCK)
    s = jnp.einsum('tkgd,tkmd->tkgm', q, sk.reshape(t, kv, nsel * SEL_BLOCK, HEAD_DIM)) * ATTN_SCALE
    p = masked_softmax(s, mask)
    return jnp.einsum('tkgm,tkmd->tkgd', p.astype(sv.dtype), sv.reshape(t, kv, nsel * SEL_BLOCK, HEAD_DIM))


def nsa_core(q, tq, ck, cv, ns, sel_fn, kw, kw_pos, gates):
    nc = ck.shape[1]
    cmp_end = jnp.arange(nc) * CMP_STRIDE + (CMP_BLOCK - 1)
    cmask = (cmp_end[None, :] <= tq[:, None])[None, :, None, None, :]
    s = jnp.einsum('btkgd,bnkd->btkgn', q, ck) * ATTN_SCALE
    p = masked_softmax(s, cmask)
    o_cmp = jnp.einsum('btkgn,bnkd->btkgd', p.astype(cv.dtype), cv)
    p_slc = cmp_to_sel(jnp.sum(p, axis=3), ns)
    j = jnp.arange(ns)[None, :]
    jt = (tq // SEL_BLOCK)[:, None]
    forced = ((j == 0) | (j == jt) | (j == jt - 1))[None, :, None, :]
    valid = (j <= jt)[None, :, None, :]
    score = jnp.where(forced, jnp.inf, jnp.where(valid, p_slc, -jnp.inf))
    _, idx = lax.top_k(score, min(N_SEL, ns))
    o_slc = sel_fn(q, idx)
    diff = tq[:, None] - kw_pos[None, :]
    wmask = ((diff >= 0) & (diff <= WINDOW) & (kw_pos >= 0)[None, :])[None, :, None, None, :]
    s3 = jnp.einsum('btkgd,bskd->btkgs', q, kw[:, :, 0]) * ATTN_SCALE
    p3 = masked_softmax(s3, wmask)
    o_win = jnp.einsum('btkgs,bskd->btkgd', p3.astype(kw.dtype), kw[:, :, 1])
    return (gates[:, :, 0, ..., None] * o_cmp + gates[:, :, 1, ..., None] * o_slc
            + gates[:, :, 2, ..., None] * o_win)


def nsa_prompt(q, kv_cmp, kv_slc, kv_win, gates, pe, w1, b1, w2):
    b, t = q.shape[:2]
    ck, cv = cmp_finish(cmp_half_proj(kv_cmp, w1), pe, w1, b1, w2)
    ns = t // SEL_BLOCK
    kv_blocks = kv_slc.reshape(b, ns, SEL_BLOCK, 2, NSA_KV, HEAD_DIM)
    kv_win_pad = jnp.pad(kv_win, ((0, 0), (WINDOW, 0), (0, 0), (0, 0), (0, 0)))
    kvi = jnp.arange(NSA_KV)[None, :, None]
    nqb = t // Q_BLOCK

    def to_blocks(a):
        return a.reshape(b, nqb, Q_BLOCK, *a.shape[2:]).swapaxes(0, 1)

    def one_block(args):
        qb, gb, blk = args
        start = blk * Q_BLOCK
        tq = start + jnp.arange(Q_BLOCK)
        kw = lax.dynamic_slice_in_dim(kv_win_pad, start, WINDOW + Q_BLOCK, axis=1)
        kw_pos = start - WINDOW + jnp.arange(WINDOW + Q_BLOCK)

        def sel_one(a):
            qq, idx, kvb = a
            g = kvb[idx, :, :, kvi]
            return sel_attn(qq, g[..., 0, :], g[..., 1, :], idx, tq)

        def sel_fn(qq, idx):
            return lax.map(sel_one, (qq, idx, kv_blocks))

        return nsa_core(qb, tq, ck, cv, ns, sel_fn, kw, kw_pos, gb)

    o = lax.map(one_block, (to_blocks(q), to_blocks(gates), jnp.arange(nqb)))
    return o.swapaxes(0, 1).reshape(b, t, NSA_KV, NSA_GROUP, HEAD_DIM)


def nsa_sample(q, kv_cmp, kv_slc, kv_win, gates, cache_cmp, cache_slc, win_buf, page_table, pe, w1, b1, w2):
    db, t = q.shape[:2]
    n_pool = cache_cmp.shape[0]
    past_len = page_table.shape[1] * PAGE_SIZE
    tq = past_len + jnp.arange(t)
    hp = lax.map(lambda pt: cmp_half_proj(cache_cmp[pt].reshape(1, past_len, 2, NSA_KV, HEAD_DIM), w1)[0],
                 page_table)
    n_new_seg = t // CMP_STRIDE
    if n_new_seg > 0:
        hp = jnp.concatenate([hp.astype(kv_cmp.dtype),
                              cmp_half_proj(kv_cmp[:, :n_new_seg * CMP_STRIDE], w1)], axis=1)
    ck, cv = cmp_finish(hp, pe, w1, b1, w2)
    ns = -(-(past_len + t) // SEL_BLOCK)
    past_blocks = past_len // SEL_BLOCK
    new_blocks = ns - past_blocks
    bpp = PAGE_SIZE // SEL_BLOCK
    pool_b = cache_slc.reshape(n_pool, bpp, SEL_BLOCK, 2, NSA_KV, HEAD_DIM)
    new_b = jnp.pad(kv_slc, ((0, 0), (0, new_blocks * SEL_BLOCK - t), (0, 0), (0, 0), (0, 0)))
    new_b = new_b.reshape(db, new_blocks, SEL_BLOCK, 2, NSA_KV, HEAD_DIM)
    kvi = jnp.arange(NSA_KV)[None, :, None]

    def sel_one(a):
        qq, idx, pt, nb = a
        jp = jnp.minimum(idx, past_blocks - 1)
        past = pool_b[pt[jp // bpp], jp % bpp, :, :, kvi]
        new = nb[jnp.clip(idx - past_blocks, 0, new_blocks - 1), :, :, kvi]
        g = jnp.where((idx < past_blocks)[..., None, None, None], past.astype(new.dtype), new)
        return sel_attn(qq, g[..., 0, :], g[..., 1, :], idx, tq)

    def sel_fn(qq, idx):
        return lax.map(sel_one, (qq, idx, page_table, new_b))

    wb = win_buf.shape[1]
    kw = jnp.concatenate([win_buf.astype(kv_win.dtype), kv_win], axis=1)
    kw_pos = past_len - wb + jnp.arange(wb + t)
    o = nsa_core(q, tq, ck, cv, ns, sel_fn, kw, kw_pos, gates)
    return o, kw[:, t:]


def hgrn2(q, f_logit, i, g, lb, s0, out_norm):
    f32 = jnp.float32
    b, t, _ = q.shape
    fg = lb + (1.0 - lb) * jax.nn.sigmoid(f_logit.astype(f32))

    def heads(a):
        return a.reshape(b, t, HG_HEADS, HG_HEAD_DIM)

    c = math.gcd(t, HG_CHUNK)
    n = t // c

    def chunks(a):
        return heads(a).reshape(b, n, c, HG_HEADS, HG_HEAD_DIM).transpose(1, 0, 3, 2, 4)

    causal = jnp.tril(jnp.ones((c, c), bool))[..., None]

    def step(S, inp):
        qc, kc, vc, gc = inp
        bc = jnp.cumsum(gc, axis=2)
        o = jnp.einsum('bhtk,bhkv->bhtv', qc * jnp.exp(bc), S)
        dec = jnp.exp(jnp.where(causal, bc[:, :, :, None, :] - bc[:, :, None, :, :], -jnp.inf))
        a = jnp.einsum('bhtk,bhsk,bhtsk->bhts', qc, kc, dec)
        o = o + jnp.einsum('bhts,bhsv->bhtv', a, vc)
        bl = bc[:, :, -1:, :]
        S = jnp.exp(bl[:, :, 0, :, None]) * S + jnp.einsum('bhsk,bhsv->bhkv', kc * jnp.exp(bl - bc), vc)
        return S, o

    xs = (chunks(jax.nn.silu(q.astype(f32)) * HG_SCALE), chunks(1.0 - fg),
          chunks(i.astype(f32)), chunks(jnp.log(fg)))
    S, o = lax.scan(step, s0.astype(f32), xs)
    o = o.transpose(1, 0, 3, 2, 4).reshape(b, t, HG_HEADS, HG_HEAD_DIM)
    o = rmsnorm(o, out_norm).reshape(b, t, HG_WIDTH) * jax.nn.silu(g.astype(f32))
    return o.astype(q.dtype), S


def mix_out(o_nsa, o_hg, nsa_norm, w_out):
    lead = o_nsa.shape[:2]
    o_nsa = rmsnorm(o_nsa.reshape(*lead, NSA_HEADS, HEAD_DIM), nsa_norm).reshape(*lead, NSA_WIDTH)
    return jnp.concatenate([o_nsa, o_hg.astype(o_nsa.dtype)], axis=-1) @ w_out


def setup_inputs(seed: int = 0) -> dict:
    key = jax.random.key(seed)
    ks = iter(jax.random.split(key, 40))
    f32 = jnp.float32

    def nrm(shape, scale):
        return jax.random.normal(next(ks), shape, f32) * scale

    def gain(shape):
        return 1.0 + nrm(shape, 0.01)

    n_pages = PAST_LEN // PAGE_SIZE
    n_pool = (DEC_BATCH * n_pages * 5) // 4
    win_buf = min(WINDOW, PAST_LEN)
    kv_pool_shape = (DEPTH, n_pool, PAGE_SIZE, 2, NSA_KV, HEAD_DIM)
    x_prompt = nrm((BATCH, SEQ, D_MODEL), 1.0)
    x_sample = nrm((DEC_BATCH, DEC_SEQ, D_MODEL), 1.0)
    cache_cmp_kv = nrm(kv_pool_shape, 1.0)
    cache_slc_kv = nrm(kv_pool_shape, 1.0)
    state_win_kv = nrm((DEPTH, DEC_BATCH, win_buf, 2, NSA_KV, HEAD_DIM), 1.0)
    state_hgrn = nrm((DEPTH, DEC_BATCH, HG_HEADS, HG_HEAD_DIM, HG_HEAD_DIM), 0.5)
    perm = jax.random.permutation(next(ks), n_pool)[:DEC_BATCH * n_pages]
    page_table = perm.reshape(DEC_BATCH, n_pages).astype(jnp.int32)
    return {
        'x_prompt': x_prompt,
        'x_sample': x_sample,
        'cache_cmp_kv': cache_cmp_kv,
        'cache_slc_kv': cache_slc_kv,
        'state_win_kv': state_win_kv,
        'state_hgrn': state_hgrn,
        'page_table': page_table,
        'ffn1_norm': gain((DEPTH, D_MODEL)),
        'ffn1_w_gate': nrm((DEPTH, D_MODEL, D_FF), D_MODEL ** -0.5),
        'ffn1_w_up': nrm((DEPTH, D_MODEL, D_FF), D_MODEL ** -0.5),
        'ffn1_w_down': nrm((DEPTH, D_FF, D_MODEL), D_FF ** -0.5),
        'mix_norm': gain((DEPTH, D_MODEL)),
        'w_in': nrm((DEPTH, D_MODEL, N_IN), D_MODEL ** -0.5),
        'cmp_pe': nrm((DEPTH, 2, CMP_BLOCK, HEAD_DIM), 0.5),
        'cmp_w1': nrm((DEPTH, 2, CMP_BLOCK, HEAD_DIM, CMP_HID), (CMP_BLOCK * HEAD_DIM) ** -0.5),
        'cmp_b1': nrm((DEPTH, 2, CMP_HID), 0.01),
        'cmp_w2': nrm((DEPTH, 2, CMP_HID, HEAD_DIM), CMP_HID ** -0.5),
        'nsa_out_norm': gain((DEPTH, NSA_HEADS, HEAD_DIM)),
        'hg_lb_logits': nrm((DEPTH + 1, HG_WIDTH), 0.5),
        'hg_out_norm': gain((DEPTH, HG_HEADS, HG_HEAD_DIM)),
        'w_out': nrm((DEPTH, MIX_WIDTH, D_MODEL), MIX_WIDTH ** -0.5),
        'ffn2_norm': gain((DEPTH, D_MODEL)),
        'ffn2_w_gate': nrm((DEPTH, D_MODEL, D_FF), D_MODEL ** -0.5),
        'ffn2_w_up': nrm((DEPTH, D_MODEL, D_FF), D_MODEL ** -0.5),
        'ffn2_w_down': nrm((DEPTH, D_FF, D_MODEL), D_FF ** -0.5),
        'final_norm': gain((D_MODEL,)),
    }


def reference(x_prompt, x_sample, cache_cmp_kv, cache_slc_kv, state_win_kv, state_hgrn, page_table,
              ffn1_norm, ffn1_w_gate, ffn1_w_up, ffn1_w_down, mix_norm, w_in,
              cmp_pe, cmp_w1, cmp_b1, cmp_w2, nsa_out_norm, hg_lb_logits, hg_out_norm, w_out,
              ffn2_norm, ffn2_w_gate, ffn2_w_up, ffn2_w_down, final_norm):
    f32 = jnp.float32
    lower_bounds = jnp.cumsum(jax.nn.softmax(hg_lb_logits.astype(f32), axis=0), axis=0)
    xp, xs = x_prompt, x_sample
    b, t = xp.shape[:2]
    p_cmp, p_slc, p_win, p_hg = [], [], [], []
    s_cmp, s_slc, s_win, s_hg = [], [], [], []
    for l in range(DEPTH):
        xp = swiglu_half_step(xp, ffn1_norm[l], ffn1_w_gate[l], ffn1_w_up[l], ffn1_w_down[l])
        q, kvc, kvs, kvw, gates, hq, hf, hi, hg = split_mix(rmsnorm(xp, mix_norm[l]) @ w_in[l])
        o_nsa = nsa_prompt(q, kvc, kvs, kvw, gates, cmp_pe[l], cmp_w1[l], cmp_b1[l], cmp_w2[l])
        s0 = jnp.zeros((b, HG_HEADS, HG_HEAD_DIM, HG_HEAD_DIM), f32)
        o_hg, hg_state = hgrn2(hq, hf, hi, hg, lower_bounds[l], s0, hg_out_norm[l])
        xp = xp + mix_out(o_nsa, o_hg, nsa_out_norm[l], w_out[l])
        xp = swiglu_half_step(xp, ffn2_norm[l], ffn2_w_gate[l], ffn2_w_up[l], ffn2_w_down[l])
        p_cmp.append(kvc)
        p_slc.append(kvs)
        p_win.append(kvw[:, t - min(WINDOW, t):])
        p_hg.append(hg_state.astype(state_hgrn.dtype))
        xs = swiglu_half_step(xs, ffn1_norm[l], ffn1_w_gate[l], ffn1_w_up[l], ffn1_w_down[l])
        q, kvc, kvs, kvw, gates, hq, hf, hi, hg = split_mix(rmsnorm(xs, mix_norm[l]) @ w_in[l])
        o_nsa, win_new = nsa_sample(q, kvc, kvs, kvw, gates, cache_cmp_kv[l], cache_slc_kv[l],
                                    state_win_kv[l], page_table, cmp_pe[l], cmp_w1[l], cmp_b1[l], cmp_w2[l])
        o_hg, hg_state = hgrn2(hq, hf, hi, hg, lower_bounds[l], state_hgrn[l], hg_out_norm[l])
        xs = xs + mix_out(o_nsa, o_hg, nsa_out_norm[l], w_out[l])
        xs = swiglu_half_step(xs, ffn2_norm[l], ffn2_w_gate[l], ffn2_w_up[l], ffn2_w_down[l])
        s_cmp.append(kvc)
        s_slc.append(kvs)
        s_win.append(win_new)
        s_hg.append(hg_state.astype(state_hgrn.dtype))
    y_prompt = rmsnorm(xp, final_norm)
    y_sample = rmsnorm(xs, final_norm)
    return (y_prompt, y_sample,
            jnp.stack(p_cmp), jnp.stack(p_slc), jnp.stack(p_win), jnp.stack(p_hg),
            jnp.stack(s_cmp), jnp.stack(s_slc), jnp.stack(s_win), jnp.stack(s_hg))
```

```python
import functools

import jax
import jax.numpy as jnp
from jax import lax
from jax.experimental import pallas as pl
from jax.experimental.pallas import tpu as pltpu

F32 = jnp.float32
BF16 = jnp.bfloat16
NEG_INF = float("-inf")

HEAD_DIM = 128
NSA_KV = 2
NSA_GROUP = 4
NSA_HEADS = NSA_KV * NSA_GROUP
NSA_WIDTH = NSA_HEADS * HEAD_DIM
KV_COLS = 2 * NSA_KV * HEAD_DIM
HG_HEADS = 8
HG_WIDTH = HG_HEADS * HEAD_DIM
CMP_BLOCK = 32
CMP_STRIDE = 16
SEL_BLOCK = 64
SEL_SHIFT = 6
SEL_RATIO = SEL_BLOCK // CMP_STRIDE
N_SEL = 16
SEL_SPAN_W = (1.0, 2.0, 2.0, 2.0, 1.0)
WINDOW = 512
PAGE_SIZE = 128
HG_CHUNK = 64
HG_SUB = 16
ATTN_SCALE = HEAD_DIM ** -0.5
HG_SCALE = HEAD_DIM ** -0.5
EPS = 1e-6
MASK_BIAS = 2.0 ** 40

ZA_COLS = NSA_WIDTH + 3 * KV_COLS
ZH_COLS = 4 * HG_WIDTH
ZG_COLS = 128
PROJ_TN = 512
COL_CMP = NSA_WIDTH
COL_SLC = NSA_WIDTH + KV_COLS
COL_WIN = NSA_WIDTH + 2 * KV_COLS

VMEM_LIMIT = 56 * 1024 * 1024


def _cparams(sem):
    return pltpu.CompilerParams(dimension_semantics=sem, vmem_limit_bytes=VMEM_LIMIT)


def _rms(x, g):
    return x * lax.rsqrt(jnp.mean(x * x, axis=-1, keepdims=True) + EPS) * g


def _dot(a, b):
    return jnp.dot(a, b, preferred_element_type=F32)


def _dot_nt(a, b):
    return lax.dot_general(a, b, (((1,), (1,)), ((), ())), preferred_element_type=F32)


def _dot_tn(a, b):
    return lax.dot_general(a, b, (((0,), (0,)), ((), ())), preferred_element_type=F32)


def _split3(x):
    hi = x.astype(BF16)
    r1 = x - hi.astype(F32)
    mid = r1.astype(BF16)
    lo = (r1 - mid.astype(F32)).astype(BF16)
    return hi, mid, lo


def _ffn_kernel(x_ref, g_ref, wg_ref, wu_ref, wd_ref, pg_ref, *rest, emit_x):
    if emit_x:
        xo_ref, no_ref, h_sc, acc_sc = rest
    else:
        no_ref, h_sc, acc_sc = rest
    j = pl.program_id(1)

    @pl.when(j == 0)
    def _():
        h_sc[...] = _rms(x_ref[...], g_ref[...]).astype(BF16)
        acc_sc[...] = jnp.zeros_like(acc_sc)

    h = h_sc[...]
    a = _dot(h, wg_ref[...])
    u = _dot(h, wu_ref[...])
    act = (a * jax.nn.sigmoid(a) * u).astype(BF16)
    acc_sc[...] += _dot(act, wd_ref[...])

    @pl.when(j == pl.num_programs(1) - 1)
    def _():
        xn = x_ref[...] + 0.5 * acc_sc[...]
        if emit_x:
            xo_ref[...] = xn
        no_ref[...] = _rms(xn, pg_ref[...]).astype(no_ref.dtype)


def _ffn(x, g, wg, wu, wd, post_g, *, emit_x, norm_dtype, tf=512):
    n, d = x.shape
    fp = wg.shape[1]
    tm = min(n, 512)
    assert n % tm == 0 and fp % tf == 0
    row = pl.BlockSpec((tm, d), lambda i, j: (i, 0))
    vec = pl.BlockSpec((1, d), lambda i, j: (0, 0))
    out_shape = [jax.ShapeDtypeStruct((n, d), norm_dtype)]
    out_specs = [row]
    if emit_x:
        out_shape = [jax.ShapeDtypeStruct((n, d), F32)] + out_shape
        out_specs = [row] + out_specs
    return pl.pallas_call(
        functools.partial(_ffn_kernel, emit_x=emit_x),
        grid=(n // tm, fp // tf),
        in_specs=[row, vec,
                  pl.BlockSpec((d, tf), lambda i, j: (0, j)),
                  pl.BlockSpec((d, tf), lambda i, j: (0, j)),
                  pl.BlockSpec((tf, d), lambda i, j: (j, 0)),
                  vec],
        out_specs=out_specs,
        out_shape=out_shape,
        scratch_shapes=[pltpu.VMEM((tm, d), BF16), pltpu.VMEM((tm, d), F32)],
        compiler_params=_cparams(("parallel", "arbitrary")),
        name="ffn_half_step",
    )(x, g.reshape(1, d), wg, wu, wd, post_g.reshape(1, d))


def _inproj_kernel(h_ref, w_ref, za_ref, za16_ref, zh_ref, zg_ref, *, na, nh):
    j = pl.program_id(1)
    r = _dot(h_ref[...], w_ref[...])

    @pl.when(j < na)
    def _():
        za_ref[...] = r
        za16_ref[...] = r.astype(BF16)

    @pl.when((j >= na) & (j < na + nh))
    def _():
        zh_ref[...] = r

    @pl.when(j == na + nh)
    def _():
        zg_ref[...] = r[:, :ZG_COLS]


def _inproj(h, w):
    n, d = h.shape
    tn = PROJ_TN
    na, nh = ZA_COLS // tn, ZH_COLS // tn
    tm = min(n, 1024)
    assert n % tm == 0 and w.shape[1] == (na + nh + 1) * tn
    return pl.pallas_call(
        functools.partial(_inproj_kernel, na=na, nh=nh),
        grid=(n // tm, na + nh + 1),
        in_specs=[pl.BlockSpec((tm, d), lambda i, j: (i, 0)),
                  pl.BlockSpec((d, tn), lambda i, j: (0, j))],
        out_specs=[pl.BlockSpec((tm, tn), lambda i, j: (i, jnp.minimum(j, na - 1))),
                   pl.BlockSpec((tm, tn), lambda i, j: (i, jnp.minimum(j, na - 1))),
                   pl.BlockSpec((tm, tn), lambda i, j: (i, jnp.clip(j - na, 0, nh - 1))),
                   pl.BlockSpec((tm, ZG_COLS), lambda i, j: (i, 0))],
        out_shape=[jax.ShapeDtypeStruct((n, ZA_COLS), F32),
                   jax.ShapeDtypeStruct((n, ZA_COLS), BF16),
                   jax.ShapeDtypeStruct((n, ZH_COLS), F32),
                   jax.ShapeDtypeStruct((n, ZG_COLS), F32)],
        compiler_params=_cparams(("parallel", "arbitrary")),
        name="in_projection",
    )(h, w)


def _outproj_kernel(x_ref, a_ref, b_ref, w_ref, o_ref):
    half = a_ref.shape[1]
    o_ref[...] = (x_ref[...]
                  + _dot(a_ref[...].astype(BF16), w_ref[0:half, :])
                  + _dot(b_ref[...].astype(BF16), w_ref[half:, :]))


def _outproj(x, a, b, w):
    n, d = x.shape
    half = a.shape[1]
    tm = min(n, 512)
    assert n % tm == 0
    return pl.pallas_call(
        _outproj_kernel,
        grid=(n // tm,),
        in_specs=[pl.BlockSpec((tm, d), lambda i: (i, 0)),
                  pl.BlockSpec((tm, half), lambda i: (i, 0)),
                  pl.BlockSpec((tm, half), lambda i: (i, 0)),
                  pl.BlockSpec((2 * half, d), lambda i: (0, 0))],
        out_specs=pl.BlockSpec((tm, d), lambda i: (i, 0)),
        out_shape=jax.ShapeDtypeStruct((n, d), F32),
        compiler_params=_cparams(("parallel",)),
        name="out_projection",
    )(x, a, b, w)


def _cmp_bias_kernel(pe_ref, w1_ref, b1_ref, o_ref):
    for c in range(2):
        pe8 = jnp.broadcast_to(pe_ref[c], (8, pe_ref.shape[2])).astype(BF16)
        o_ref[c] = _dot(pe8, w1_ref[c])[0:1, :] + b1_ref[c]


def _cmp_bias(pe, w1, b1):
    flat = CMP_BLOCK * HEAD_DIM
    return pl.pallas_call(
        _cmp_bias_kernel,
        out_shape=jax.ShapeDtypeStruct((2, 1, HEAD_DIM), F32),
        name="cmp_bias",
    )(pe.reshape(2, 1, flat), w1.reshape(2, flat, HEAD_DIM).astype(BF16), b1.reshape(2, 1, HEAD_DIM))


def _cmp_accumulate(get_rows, w1r_ref, hp_sc, seg0, nss):
    for c in range(2):
        acc = None
        for r in range(CMP_STRIDE):
            xr = jnp.concatenate([get_rows(r, 2 * c + k) for k in range(NSA_KV)], axis=0).astype(BF16)
            d = _dot(xr, w1r_ref[c, r])
            acc = d if acc is None else acc + d
        for k in range(NSA_KV):
            hp_sc[2 * c + k, pl.ds(seg0, nss), :] = acc[k * nss:(k + 1) * nss]


def _cmp_finish(hp_sc, bias_ref, w2_ref, o_ref):
    nseg = hp_sc.shape[1]
    last = lax.broadcasted_iota(jnp.int32, (nseg, 1), 0) == nseg - 1
    for ck in range(2 * NSA_KV):
        c = ck // NSA_KV
        first = hp_sc[ck, :, 0:HEAD_DIM]
        second = pltpu.roll(hp_sc[ck, :, HEAD_DIM:2 * HEAD_DIM], nseg - 1, 0)
        pre = first + second + bias_ref[c]
        out = _dot((pre * jax.nn.sigmoid(pre)).astype(BF16), w2_ref[c])
        o_ref[0, ck] = jnp.where(last, 0.0, out).astype(o_ref.dtype)


def _cmp_prompt_kernel(x0_ref, x1_ref, x2_ref, x3_ref, w1r_ref, bias_ref, w2_ref, o_ref, hp_sc, *, nss):
    j = pl.program_id(1)
    x_refs = (x0_ref, x1_ref, x2_ref, x3_ref)

    def get_rows(r, ck):
        return x_refs[ck][pl.ds(r, nss, stride=CMP_STRIDE), :]

    _cmp_accumulate(get_rows, w1r_ref, hp_sc, pl.multiple_of(j * nss, nss), nss)

    @pl.when(j == pl.num_programs(1) - 1)
    def _():
        _cmp_finish(hp_sc, bias_ref, w2_ref, o_ref)


def _cmp_weight_specs(grid_rank):
    zeros = lambda n: (lambda *a: (0,) * n)
    return [pl.BlockSpec((2, CMP_STRIDE, HEAD_DIM, 2 * HEAD_DIM), zeros(4)),
            pl.BlockSpec((2, 1, HEAD_DIM), zeros(3)),
            pl.BlockSpec((2, HEAD_DIM, HEAD_DIM), zeros(3))]


def _cmp_prompt(za, b, t, w1r, bias, w2):
    rc = min(t, 2048)
    nss = rc // CMP_STRIDE
    nseg = t // CMP_STRIDE
    steps = t // rc
    assert t % rc == 0
    return pl.pallas_call(
        functools.partial(_cmp_prompt_kernel, nss=nss),
        grid=(b, steps),
        in_specs=[pl.BlockSpec((rc, HEAD_DIM), functools.partial(lambda i, j, ck: (i * steps + j, COL_CMP // HEAD_DIM + ck), ck=ck))
                  for ck in range(2 * NSA_KV)] + _cmp_weight_specs(2),
        out_specs=pl.BlockSpec((1, 2 * NSA_KV, nseg, HEAD_DIM), lambda i, j: (i, 0, 0, 0)),
        out_shape=jax.ShapeDtypeStruct((b, 2 * NSA_KV, nseg, HEAD_DIM), BF16),
        scratch_shapes=[pltpu.VMEM((2 * NSA_KV, nseg, 2 * HEAD_DIM), F32)],
        compiler_params=_cparams(("parallel", "arbitrary")),
        name="cmp_mlp_prompt",
    )(za, za, za, za, w1r, bias, w2)


CMP_PAGES = 16


def _cmp_paged_kernel(pt_ref, *refs):
    pages = refs[:CMP_PAGES]
    w1r_ref, bias_ref, w2_ref, o_ref, hp_sc, x_sc = refs[CMP_PAGES:]
    j = pl.program_id(1)
    nss = CMP_PAGES * PAGE_SIZE // CMP_STRIDE
    for g, pg in enumerate(pages):
        for ck in range(2 * NSA_KV):
            x_sc[ck, g * PAGE_SIZE:(g + 1) * PAGE_SIZE, :] = pg[0, :, ck * HEAD_DIM:(ck + 1) * HEAD_DIM]

    def get_rows(r, ck):
        return x_sc[ck, pl.ds(r, nss, stride=CMP_STRIDE), :]

    _cmp_accumulate(get_rows, w1r_ref, hp_sc, pl.multiple_of(j * nss, nss), nss)

    @pl.when(j == pl.num_programs(1) - 1)
    def _():
        _cmp_finish(hp_sc, bias_ref, w2_ref, o_ref)


def _page_specs(n):
    def spec(g):
        return pl.BlockSpec((1, PAGE_SIZE, KV_COLS), lambda i, j, pt: (pt[i, j * n + g], 0, 0))
    return [spec(g) for g in range(n)]


def _cmp_paged(cache, page_table, w1r, bias, w2):
    db, n_pages = page_table.shape
    assert n_pages % CMP_PAGES == 0
    nseg = n_pages * PAGE_SIZE // CMP_STRIDE
    zeros = lambda n: (lambda i, j, pt: (0,) * n)
    grid_spec = pltpu.PrefetchScalarGridSpec(
        num_scalar_prefetch=1,
        grid=(db, n_pages // CMP_PAGES),
        in_specs=_page_specs(CMP_PAGES) + [
            pl.BlockSpec((2, CMP_STRIDE, HEAD_DIM, 2 * HEAD_DIM), zeros(4)),
            pl.BlockSpec((2, 1, HEAD_DIM), zeros(3)),
            pl.BlockSpec((2, HEAD_DIM, HEAD_DIM), zeros(3))],
        out_specs=pl.BlockSpec((1, 2 * NSA_KV, nseg, HEAD_DIM), lambda i, j, pt: (i, 0, 0, 0)),
        scratch_shapes=[pltpu.VMEM((2 * NSA_KV, nseg, 2 * HEAD_DIM), F32),
                        pltpu.VMEM((2 * NSA_KV, CMP_PAGES * PAGE_SIZE, HEAD_DIM), F32)])
    return pl.pallas_call(
        _cmp_paged_kernel,
        grid_spec=grid_spec,
        out_shape=jax.ShapeDtypeStruct((db, 2 * NSA_KV, nseg, HEAD_DIM), BF16),
        compiler_params=_cparams(("parallel", "arbitrary")),
        name="cmp_mlp_paged",
    )(page_table, *([cache] * CMP_PAGES), w1r, bias, w2)


def _softmax0(s):
    m = jnp.max(s, axis=0, keepdims=True)
    m = jnp.where(m > NEG_INF, m, 0.0)
    e = jnp.exp(s - m)
    l = jnp.sum(e, axis=0, keepdims=True)
    return e * (1.0 / jnp.maximum(l, 1e-30))


def _cmp_attn_t(ck, cv, q, tpos):
    ncp = ck.shape[0]
    s = _dot_nt(ck, q) * ATTN_SCALE
    cend = lax.broadcasted_iota(jnp.int32, (ncp, 1), 0) * CMP_STRIDE + (CMP_BLOCK - 1)
    p = _softmax0(jnp.where(cend <= tpos, s, NEG_INF))
    return p, _dot_tn(cv, p.astype(BF16))


def _sel_importance_t(msel, u):
    hi, mid, lo = _split3(u)
    return _dot(msel, hi) + _dot(msel, mid) + _dot(msel, lo)


def _topk_t(pslc, tpos, nsel):
    nsp = pslc.shape[0]
    jio = lax.broadcasted_iota(jnp.int32, pslc.shape, 0)
    jt = lax.shift_right_logical(tpos, SEL_SHIFT)
    forced = (jio == 0) | (jio == jt) | (jio == jt - 1)
    score = jnp.where(forced, jnp.inf, jnp.where(jio <= jt, pslc, NEG_INF))
    sel = jnp.zeros(pslc.shape, F32)
    for _ in range(nsel):
        m = jnp.max(score, axis=0, keepdims=True)
        idx = jnp.min(jnp.where(score == m, jio, nsp), axis=0, keepdims=True)
        hit = jio == idx
        score = jnp.where(hit, NEG_INF, score)
        sel = jnp.where(hit, 1.0, sel)
    return sel


def _nsa_prompt_kernel(q_ref, ks_ref, vs_ref, kw_ref, vw_ref, ck_ref, cv_ref, zg_ref, msel_ref, gain_ref,
                       o_ref, gate_sc, *, tq, tk, nsel):
    k = pl.program_id(1)
    i = pl.program_id(2)
    t0 = i * tq
    r = NSA_GROUP * tq
    q = q_ref[...]
    qr = jnp.concatenate([q[:, g * HEAD_DIM:(g + 1) * HEAD_DIM] for g in range(NSA_GROUP)], axis=0)
    tpos1 = t0 + lax.broadcasted_iota(jnp.int32, (1, tq), 1)
    tpos = jnp.concatenate([tpos1] * NSA_GROUP, axis=1)

    p, o_cmp = _cmp_attn_t(ck_ref[0, 0], cv_ref[0, 0], qr, tpos)
    imp = p[:, 0:tq]
    for g in range(1, NSA_GROUP):
        imp = imp + p[:, g * tq:(g + 1) * tq]
    sel = _topk_t(_sel_importance_t(msel_ref[...], imp), tpos1, nsel)

    selm1 = (sel - 1.0).T.astype(BF16)
    q_aug = jnp.concatenate([qr, jnp.concatenate([selm1] * NSA_GROUP, axis=0)], axis=1)
    nsp = selm1.shape[1]
    blk_lane = lax.broadcasted_iota(jnp.int32, (1, nsp), 1)

    def step(kt, carry):
        m, l, acc = carry
        off = pl.multiple_of(kt * tk, tk)
        spos = off + lax.broadcasted_iota(jnp.int32, (tk, 1), 0)
        onehot = jnp.where(lax.shift_right_logical(spos, SEL_SHIFT) == blk_lane, MASK_BIAS, 0.0).astype(BF16)
        k_aug = jnp.concatenate([ks_ref[pl.ds(off, tk), :], onehot], axis=1)
        s = _dot_nt(k_aug, q_aug) * ATTN_SCALE
        s = jnp.where(spos <= tpos, s, NEG_INF)
        m_new = jnp.maximum(m, jnp.max(s, axis=0, keepdims=True))
        alpha = jnp.exp(m - m_new)
        e = jnp.exp(s - m_new)
        l = alpha * l + jnp.sum(e, axis=0, keepdims=True)
        acc = alpha * acc + _dot_tn(vs_ref[pl.ds(off, tk), :], e.astype(BF16))
        return m_new, l, acc

    n_kt = (t0 + tq - 1) // tk + 1
    init = (jnp.full((1, r), NEG_INF, F32), jnp.zeros((1, r), F32), jnp.zeros((HEAD_DIM, r), F32))
    _, l, acc = lax.fori_loop(0, n_kt, step, init)
    o_slc = acc * (1.0 / jnp.maximum(l, 1e-30))

    wlen = WINDOW + tq
    ws = pl.multiple_of(jnp.maximum(t0 - WINDOW, 0), tq)
    d = tpos - (ws + lax.broadcasted_iota(jnp.int32, (wlen, 1), 0))
    s = _dot_nt(kw_ref[pl.ds(ws, wlen), :], qr) * ATTN_SCALE
    pw = _softmax0(jnp.where((d >= 0) & (d <= WINDOW), s, NEG_INF))
    o_win = _dot_tn(vw_ref[pl.ds(ws, wlen), :], pw.astype(BF16))

    gate_sc[...] = jax.nn.sigmoid(zg_ref[...].T[0:gate_sc.shape[0], :])

    def gate_row(branch):
        return jnp.concatenate(
            [gate_sc[pl.ds(branch * NSA_HEADS + k * NSA_GROUP + g, 1), :] for g in range(NSA_GROUP)], axis=1)

    o = gate_row(0) * o_cmp + gate_row(1) * o_slc + gate_row(2) * o_win
    o = o * lax.rsqrt(jnp.mean(o * o, axis=0, keepdims=True) + EPS)
    for g in range(NSA_GROUP):
        og = o[:, g * tq:(g + 1) * tq] * gain_ref[k * NSA_GROUP + g]
        o_ref[:, g * HEAD_DIM:(g + 1) * HEAD_DIM] = og.T.astype(o_ref.dtype)


def _nsa_prompt(za16, ckv, zg, msel, gain_b, b, t):
    tq = 128
    tk = min(t, 512)
    nq = t // tq
    ncp = ckv.shape[2]
    nsp = msel.shape[0]
    qw = NSA_GROUP * HEAD_DIM
    assert t % tk == 0 and t >= WINDOW + tq and nsp == 128
    col = lambda base, kk: (base // HEAD_DIM) + kk

    def seq_spec(base):
        return pl.BlockSpec((t, HEAD_DIM), lambda bi, k, i: (bi, col(base, k)))

    return pl.pallas_call(
        functools.partial(_nsa_prompt_kernel, tq=tq, tk=tk, nsel=min(N_SEL, t // SEL_BLOCK)),
        grid=(b, NSA_KV, nq),
        in_specs=[pl.BlockSpec((tq, qw), lambda bi, k, i: (bi * nq + i, k)),
                  seq_spec(COL_SLC), seq_spec(COL_SLC + NSA_KV * HEAD_DIM),
                  seq_spec(COL_WIN), seq_spec(COL_WIN + NSA_KV * HEAD_DIM),
                  pl.BlockSpec((1, 1, ncp, HEAD_DIM), lambda bi, k, i: (bi, k, 0, 0)),
                  pl.BlockSpec((1, 1, ncp, HEAD_DIM), lambda bi, k, i: (bi, NSA_KV + k, 0, 0)),
                  pl.BlockSpec((tq, ZG_COLS), lambda bi, k, i: (bi * nq + i, 0)),
                  pl.BlockSpec((nsp, ncp), lambda bi, k, i: (0, 0)),
                  pl.BlockSpec((NSA_HEADS, HEAD_DIM, tq), lambda bi, k, i: (0, 0, 0))],
        out_specs=pl.BlockSpec((tq, qw), lambda bi, k, i: (bi * nq + i, k)),
        out_shape=jax.ShapeDtypeStruct((b * t, NSA_WIDTH), BF16),
        scratch_shapes=[pltpu.VMEM((32, tq), F32)],
        compiler_params=_cparams(("parallel", "parallel", "arbitrary")),
        name="nsa_prompt",
    )(za16, za16, za16, za16, za16, ckv, ckv, zg, msel, gain_b)


SLC_PAGES = 16


def _nsa_sample_kernel(pt_ref, *refs, ts, past_len):
    pages = refs[:SLC_PAGES]
    (z_ref, zg_ref, ckv_ref, win_ref, msel_ref, e_ref, gain_ref,
     o_ref, sel_sc, m_sc, l_sc, acc_sc, ocmp_sc) = refs[SLC_PAGES:]
    j = pl.program_id(1)
    rows = NSA_GROUP * ts
    lanes = 128
    step_keys = SLC_PAGES * PAGE_SIZE
    step_blocks = step_keys // SEL_BLOCK
    z = z_ref[...]

    def q_rows(k):
        return jnp.concatenate([z[:, (k * NSA_GROUP + g) * HEAD_DIM:(k * NSA_GROUP + g + 1) * HEAD_DIM]
                                for g in range(NSA_GROUP)], axis=0).astype(BF16)

    @pl.when(j == 0)
    def _():
        lane = lax.broadcasted_iota(jnp.int32, (1, lanes), 1)
        tpos = past_len + jnp.bitwise_and(lane, ts - 1)
        for k in range(NSA_KV):
            qp = jnp.concatenate([q_rows(k), jnp.zeros((lanes - rows, HEAD_DIM), BF16)], axis=0)
            p, o_cmp = _cmp_attn_t(ckv_ref[0, k], ckv_ref[0, NSA_KV + k], qp, tpos)
            ocmp_sc[k] = o_cmp.T[0:rows, :]
            u = _sel_importance_t(msel_ref[...], p)
            pslc = u
            for g in range(1, NSA_GROUP):
                pslc = pslc + pltpu.roll(u, lanes - g * ts, 1)
            sel_sc[k] = _topk_t(pslc, tpos, N_SEL)
            m_sc[k] = jnp.full((rows, 1), NEG_INF, F32)
            l_sc[k] = jnp.zeros((rows, 1), F32)
            acc_sc[k] = jnp.zeros((rows, HEAD_DIM), F32)

    def online(k, s, v):
        m_old = m_sc[k]
        m_new = jnp.maximum(m_old, jnp.max(s, axis=1, keepdims=True))
        alpha = jnp.exp(m_old - m_new)
        e = jnp.exp(s - m_new)
        l_sc[k] = alpha * l_sc[k] + jnp.sum(e, axis=1, keepdims=True)
        acc_sc[k] = alpha * acc_sc[k] + _dot(e.astype(BF16), v)
        m_sc[k] = m_new

    for k in range(NSA_KV):
        kt = jnp.concatenate([pg[0, :, k * HEAD_DIM:(k + 1) * HEAD_DIM] for pg in pages], axis=0).astype(BF16)
        vt = jnp.concatenate([pg[0, :, (NSA_KV + k) * HEAD_DIM:(NSA_KV + k + 1) * HEAD_DIM] for pg in pages],
                             axis=0).astype(BF16)
        selj = sel_sc[k, pl.ds(pl.multiple_of(j * step_blocks, step_blocks), step_blocks), :]
        mask_t = _dot_tn(selj.astype(BF16), e_ref[...])[0:ts, :]
        mask = jnp.concatenate([mask_t] * NSA_GROUP, axis=0) > 0.5
        s = _dot_nt(q_rows(k), kt) * ATTN_SCALE
        online(k, jnp.where(mask, s, NEG_INF), vt)

    @pl.when(j == pl.num_programs(1) - 1)
    def _():
        row_t = jnp.bitwise_and(lax.broadcasted_iota(jnp.int32, (rows, 1), 0), ts - 1)
        new_i = lax.broadcasted_iota(jnp.int32, (1, 2 * ts), 1)
        new_ok = (new_i < ts) & (new_i <= row_t)
        pad = jnp.zeros((ts, HEAD_DIM), F32)

        def new_rows(col):
            return jnp.concatenate([z[:, col:col + HEAD_DIM], pad], axis=0).astype(BF16)

        wlen = win_ref.shape[1]
        wpos = past_len - wlen + lax.broadcasted_iota(jnp.int32, (1, wlen), 1)
        dw = (past_len + row_t) - wpos
        win_ok = (dw >= 0) & (dw <= WINDOW) & (wpos >= 0)
        zg = zg_ref[...]
        for k in range(NSA_KV):
            q = q_rows(k)
            s_new = _dot_nt(q, new_rows(COL_SLC + k * HEAD_DIM)) * ATTN_SCALE
            online(k, jnp.where(new_ok, s_new, NEG_INF), new_rows(COL_SLC + (NSA_KV + k) * HEAD_DIM))
            o_slc = acc_sc[k] * (1.0 / jnp.maximum(l_sc[k], 1e-30))
            kw = win_ref[0, :, k * HEAD_DIM:(k + 1) * HEAD_DIM].astype(BF16)
            vw = win_ref[0, :, (NSA_KV + k) * HEAD_DIM:(NSA_KV + k + 1) * HEAD_DIM].astype(BF16)
            s1 = jnp.where(win_ok, _dot_nt(q, kw) * ATTN_SCALE, NEG_INF)
            s2 = jnp.where(new_ok, _dot_nt(q, new_rows(COL_WIN + k * HEAD_DIM)) * ATTN_SCALE, NEG_INF)
            mw = jnp.maximum(jnp.max(s1, axis=1, keepdims=True), jnp.max(s2, axis=1, keepdims=True))
            e1 = jnp.exp(s1 - mw)
            e2 = jnp.exp(s2 - mw)
            lw = jnp.sum(e1, axis=1, keepdims=True) + jnp.sum(e2, axis=1, keepdims=True)
            o_win = (_dot(e1.astype(BF16), vw)
                     + _dot(e2.astype(BF16), new_rows(COL_WIN + (NSA_KV + k) * HEAD_DIM))) * (1.0 / jnp.maximum(lw, 1e-30))

            def gate(branch):
                c0 = branch * NSA_HEADS + k * NSA_GROUP
                return jnp.concatenate([jax.nn.sigmoid(zg[:, c0 + g:c0 + g + 1]) for g in range(NSA_GROUP)], axis=0)

            o = gate(0) * ocmp_sc[k] + gate(1) * o_slc + gate(2) * o_win
            o = o * lax.rsqrt(jnp.mean(o * o, axis=1, keepdims=True) + EPS)
            for g in range(NSA_GROUP):
                h = k * NSA_GROUP + g
                o_ref[:, h * HEAD_DIM:(h + 1) * HEAD_DIM] = o[g * ts:(g + 1) * ts, :] * gain_ref[h:h + 1, :]


def _nsa_sample(za, zg, ckv, cache_slc, win, page_table, msel, e_blocks, gain, ts):
    db, n_pages = page_table.shape
    past_len = n_pages * PAGE_SIZE
    ncp = ckv.shape[2]
    nsp = msel.shape[0]
    wlen = win.shape[1]
    rows = NSA_GROUP * ts
    step_keys = SLC_PAGES * PAGE_SIZE
    assert n_pages % SLC_PAGES == 0 and ts & (ts - 1) == 0 and ts % 8 == 0 and rows <= 128
    assert past_len % SEL_BLOCK == 0 and ts <= SEL_BLOCK and ts < CMP_STRIDE
    zeros = lambda n: (lambda i, j, pt: (0,) * n)
    grid_spec = pltpu.PrefetchScalarGridSpec(
        num_scalar_prefetch=1,
        grid=(db, n_pages // SLC_PAGES),
        in_specs=_page_specs(SLC_PAGES) + [
            pl.BlockSpec((ts, ZA_COLS), lambda i, j, pt: (i, 0)),
            pl.BlockSpec((ts, ZG_COLS), lambda i, j, pt: (i, 0)),
            pl.BlockSpec((1, 2 * NSA_KV, ncp, HEAD_DIM), lambda i, j, pt: (i, 0, 0, 0)),
            pl.BlockSpec((1, wlen, KV_COLS), lambda i, j, pt: (i, 0, 0)),
            pl.BlockSpec((nsp, ncp), zeros(2)),
            pl.BlockSpec((step_keys // SEL_BLOCK, step_keys), zeros(2)),
            pl.BlockSpec((NSA_HEADS, HEAD_DIM), zeros(2))],
        out_specs=pl.BlockSpec((ts, NSA_WIDTH), lambda i, j, pt: (i, 0)),
        scratch_shapes=[pltpu.VMEM((NSA_KV, nsp, 128), F32),
                        pltpu.VMEM((NSA_KV, rows, 1), F32),
                        pltpu.VMEM((NSA_KV, rows, 1), F32),
                        pltpu.VMEM((NSA_KV, rows, HEAD_DIM), F32),
                        pltpu.VMEM((NSA_KV, rows, HEAD_DIM), F32)])
    return pl.pallas_call(
        functools.partial(_nsa_sample_kernel, ts=ts, past_len=past_len),
        grid_spec=grid_spec,
        out_shape=jax.ShapeDtypeStruct((db * ts, NSA_WIDTH), F32),
        compiler_params=_cparams(("parallel", "arbitrary")),
        name="nsa_sample",
    )(page_table, *([cache_slc] * SLC_PAGES), za, zg, ckv, win, msel, e_blocks, gain)


def _cumsum0(x):
    n = x.shape[0]
    row = lax.broadcasted_iota(jnp.int32, (n, 1), 0)
    sh = 1
    while sh < n:
        x = x + jnp.where(row >= sh, pltpu.roll(x, sh, 0), 0.0)
        sh *= 2
    return x


def _hgrn_kernel(q_ref, f_ref, i_ref, g_ref, lbl_ref, on_ref, s0_ref, o_ref, so_ref, st_sc, *, c, sb, layer):
    j = pl.program_id(2)

    @pl.when(j == 0)
    def _():
        st_sc[...] = s0_ref[0, 0].T

    lbl = lbl_ref[...]
    e = jnp.exp(lbl - jnp.max(lbl, axis=0, keepdims=True))
    lb = jnp.sum(e[0:layer + 1], axis=0, keepdims=True) / jnp.sum(e, axis=0, keepdims=True)
    gain = on_ref[...]
    nsb = c // sb
    sub_row = lax.broadcasted_iota(jnp.int32, (sb, 1), 0)

    def chunk(ci, carry):
        r0 = pl.multiple_of(ci * c, c)
        rows = pl.ds(r0, c)
        hq = q_ref[rows, :]
        fg = lb + (1.0 - lb) * jax.nn.sigmoid(f_ref[rows, :])
        kk = 1.0 - fg
        v = i_ref[rows, :]
        bc = _cumsum0(jnp.log(fg))
        qh = hq * jax.nn.sigmoid(hq) * HG_SCALE
        st = st_sc[...]
        o_inter = _dot_nt((qh * jnp.exp(bc)).astype(BF16), st.astype(BF16))
        v16 = v.astype(BF16)
        parts = []
        for si in range(nsb):
            lo, hi = si * sb, (si + 1) * sb
            bci, qi, ki, vi = bc[lo:hi], qh[lo:hi], kk[lo:hi], v[lo:hi]
            od = jnp.zeros((sb, HEAD_DIM), F32)
            for s in range(sb):
                w = jnp.exp(jnp.where(sub_row >= s, bci - bci[s:s + 1], NEG_INF))
                a = jnp.sum(qi * ki[s:s + 1] * w, axis=1, keepdims=True)
                od = od + a * vi[s:s + 1]
            if si > 0:
                bcr = bc[lo - 1:lo]
                qs = (qi * jnp.exp(bci - bcr)).astype(BF16)
                kp = (kk[0:lo] * jnp.exp(bcr - bc[0:lo])).astype(BF16)
                od = od + _dot(_dot_nt(qs, kp).astype(BF16), v16[0:lo])
            parts.append(od)
        o = o_inter + (parts[0] if nsb == 1 else jnp.concatenate(parts, axis=0))
        bl = bc[c - 1:c]
        st_sc[...] = st * jnp.exp(bl) + _dot_tn(v16, (kk * jnp.exp(bl - bc)).astype(BF16))
        gt = g_ref[rows, :]
        y = _rms(o, gain) * (gt * jax.nn.sigmoid(gt))
        o_ref[rows, :] = y.astype(o_ref.dtype)
        return carry

    lax.fori_loop(0, q_ref.shape[0] // c, chunk, 0)

    @pl.when(j == pl.num_programs(2) - 1)
    def _():
        so_ref[0, 0] = st_sc[...].T


def _hgrn(zh, lb_logits, out_norm, s0, b, t, layer, out_dtype):
    c = HG_CHUNK if t % HG_CHUNK == 0 else t
    sb = min(HG_SUB, c)
    tc = min(t, 512)
    nt = t // tc
    assert t % tc == 0 and tc % c == 0 and c % sb == 0 and c % 8 == 0

    def piece(p):
        return pl.BlockSpec((tc, HEAD_DIM), lambda bi, h, j: (bi * nt + j, p * HG_HEADS + h))

    st_spec = pl.BlockSpec((1, 1, HEAD_DIM, HEAD_DIM), lambda bi, h, j: (bi, h, 0, 0))
    nl = lb_logits.shape[0]
    return pl.pallas_call(
        functools.partial(_hgrn_kernel, c=c, sb=sb, layer=layer),
        grid=(b, HG_HEADS, nt),
        in_specs=[piece(0), piece(1), piece(2), piece(3),
                  pl.BlockSpec((nl, HEAD_DIM), lambda bi, h, j: (0, h)),
                  pl.BlockSpec((1, HEAD_DIM), lambda bi, h, j: (0, h)),
                  st_spec],
        out_specs=[pl.BlockSpec((tc, HEAD_DIM), lambda bi, h, j: (bi * nt + j, h)), st_spec],
        out_shape=[jax.ShapeDtypeStruct((b * t, HG_WIDTH), out_dtype),
                   jax.ShapeDtypeStruct((b, HG_HEADS, HEAD_DIM, HEAD_DIM), F32)],
        scratch_shapes=[pltpu.VMEM((HEAD_DIM, HEAD_DIM), F32)],
        compiler_params=_cparams(("parallel", "parallel", "arbitrary")),
        name="hgrn2",
    )(zh, zh, zh, zh, lb_logits, out_norm.reshape(1, HG_WIDTH), s0)


def _sel_coverage(ns_pad, ns, ncp, nc):
    j = jnp.arange(ns_pad)[:, None]
    i = jnp.arange(ncp)[None, :]
    d = i - (SEL_RATIO * j - 1)
    w = jnp.zeros((ns_pad, ncp), F32)
    for dd, ww in enumerate(SEL_SPAN_W):
        w = jnp.where(d == dd, ww, w)
    return jnp.where((j < ns) & (i < nc), w, 0.0).astype(BF16)


def _pack_w_in(w):
    gates = w[:, ZA_COLS:ZA_COLS + 3 * NSA_HEADS]
    return jnp.concatenate([w[:, :ZA_COLS], w[:, ZA_COLS + 3 * NSA_HEADS:],
                            jnp.pad(gates, ((0, 0), (0, PROJ_TN - 3 * NSA_HEADS)))], axis=1).astype(BF16)


def _pack_ffn(wg, wu, wd, tf=512):
    f = wg.shape[1]
    fp = -(-f // tf) * tf
    padc = ((0, 0), (0, fp - f))
    return (jnp.pad(wg, padc).astype(BF16), jnp.pad(wu, padc).astype(BF16),
            jnp.pad(wd, ((0, fp - f), (0, 0))).astype(BF16))


def kernel(x_prompt, x_sample, cache_cmp_kv, cache_slc_kv, state_win_kv, state_hgrn, page_table, ffn1_norm, ffn1_w_gate, ffn1_w_up, ffn1_w_down, mix_norm, w_in, cmp_pe, cmp_w1, cmp_b1, cmp_w2, nsa_out_norm, hg_lb_logits, hg_out_norm, w_out, ffn2_norm, ffn2_w_gate, ffn2_w_up, ffn2_w_down, final_norm):
    depth = w_in.shape[0]
    assert depth == 1, "single-layer trunk"
    l = 0
    b, t, d = x_prompt.shape
    db, ts, _ = x_sample.shape
    n_pool = cache_cmp_kv.shape[1]
    n_pages = page_table.shape[1]
    past_len = n_pages * PAGE_SIZE
    assert state_win_kv.shape[2] == min(WINDOW, past_len) and t % SEL_BLOCK == 0

    f1 = _pack_ffn(ffn1_w_gate[l], ffn1_w_up[l], ffn1_w_down[l])
    f2 = _pack_ffn(ffn2_w_gate[l], ffn2_w_up[l], ffn2_w_down[l])
    w_in_p = _pack_w_in(w_in[l])
    w_out_p = w_out[l].astype(BF16)
    w1 = cmp_w1[l]
    w1r = (w1.reshape(2, CMP_BLOCK // CMP_STRIDE, CMP_STRIDE, HEAD_DIM, HEAD_DIM)
           .transpose(0, 2, 3, 1, 4).reshape(2, CMP_STRIDE, HEAD_DIM, 2 * HEAD_DIM).astype(BF16))
    w2 = cmp_w2[l].astype(BF16)
    cmp_bias = _cmp_bias(cmp_pe[l], w1, cmp_b1[l])
    gain = nsa_out_norm[l]

    def trunk_front(x):
        x1, h = _ffn(x, ffn1_norm[l], *f1, mix_norm[l], emit_x=True, norm_dtype=BF16)
        return (x1,) + tuple(_inproj(h, w_in_p))

    def trunk_back(x1, o_nsa, o_hg):
        x2 = _outproj(x1, o_nsa, o_hg, w_out_p)
        return _ffn(x2, ffn2_norm[l], *f2, final_norm, emit_x=False, norm_dtype=F32)[0]

    def kv_rows(za, col, lead):
        return za[:, col:col + KV_COLS].reshape(*lead, 2, NSA_KV, HEAD_DIM)

    x1, za, za16, zh, zg = trunk_front(x_prompt.reshape(b * t, d))
    ns = t // SEL_BLOCK
    ncp = t // CMP_STRIDE
    msel = _sel_coverage(128, ns, ncp, ncp - 1)
    ckv = _cmp_prompt(za, b, t, w1r, cmp_bias, w2)
    gain_b = jnp.broadcast_to(gain[:, :, None], (NSA_HEADS, HEAD_DIM, 128))
    o_nsa = _nsa_prompt(za16, ckv, zg, msel, gain_b, b, t)
    s0 = jnp.zeros((b, HG_HEADS, HEAD_DIM, HEAD_DIM), F32)
    o_hg, hg_p = _hgrn(zh, hg_lb_logits, hg_out_norm[l], s0, b, t, l, BF16)
    y_prompt = trunk_back(x1, o_nsa, o_hg).reshape(b, t, d)
    p_cmp = kv_rows(za, COL_CMP, (b, t))
    p_slc = kv_rows(za, COL_SLC, (b, t))
    p_win = kv_rows(za, COL_WIN, (b, t))[:, t - min(WINDOW, t):]

    x1s, zas, _, zhs, zgs = trunk_front(x_sample.reshape(db * ts, d))
    ncp_s = past_len // CMP_STRIDE
    ns_s = -(-(past_len + ts) // SEL_BLOCK)
    ns_pad = -(-ns_s // 8) * 8
    msel_s = _sel_coverage(ns_pad, ns_s, ncp_s, ncp_s - 1)
    step_keys = SLC_PAGES * PAGE_SIZE
    e_blocks = (jnp.arange(step_keys)[None, :] // SEL_BLOCK == jnp.arange(step_keys // SEL_BLOCK)[:, None]).astype(BF16)
    ckv_s = _cmp_paged(cache_cmp_kv[l].reshape(n_pool, PAGE_SIZE, KV_COLS), page_table, w1r, cmp_bias, w2)
    win = state_win_kv[l].reshape(db, -1, KV_COLS)
    o_nsa_s = _nsa_sample(zas, zgs, ckv_s, cache_slc_kv[l].reshape(n_pool, PAGE_SIZE, KV_COLS), win,
                          page_table, msel_s, e_blocks, gain, ts)
    o_hg_s, hg_s = _hgrn(zhs, hg_lb_logits, hg_out_norm[l], state_hgrn[l], db, ts, l, F32)
    y_sample = trunk_back(x1s, o_nsa_s, o_hg_s).reshape(db, ts, d)
    s_cmp = kv_rows(zas, COL_CMP, (db, ts))
    s_slc = kv_rows(zas, COL_SLC, (db, ts))
    s_win_new = kv_rows(zas, COL_WIN, (db, ts))
    s_win = jnp.concatenate([state_win_kv[l].astype(F32), s_win_new], axis=1)[:, ts:]

    stack = lambda a: a[None]
    return (y_prompt, y_sample, stack(p_cmp), stack(p_slc), stack(p_win), hg_p[None].astype(state_hgrn.dtype),
            stack(s_cmp), stack(s_slc), stack(s_win), hg_s[None].astype(state_hgrn.dtype))
```

```python
import functools

import jax
import jax.numpy as jnp
from jax import lax
from jax.experimental import pallas as pl
from jax.experimental.pallas import tpu as pltpu

F32 = jnp.float32
BF16 = jnp.bfloat16
NEG_INF = float("-inf")

HEAD_DIM = 128
NSA_KV = 2
NSA_GROUP = 4
NSA_HEADS = NSA_KV * NSA_GROUP
NSA_WIDTH = NSA_HEADS * HEAD_DIM
KV_COLS = 2 * NSA_KV * HEAD_DIM
HG_HEADS = 8
HG_WIDTH = HG_HEADS * HEAD_DIM
CMP_BLOCK = 32
CMP_STRIDE = 16
SEL_BLOCK = 64
SEL_SHIFT = 6
SEL_RATIO = SEL_BLOCK // CMP_STRIDE
N_SEL = 16
SEL_SPAN_W = (1.0, 2.0, 2.0, 2.0, 1.0)
WINDOW = 512
PAGE_SIZE = 128
HG_CHUNK = 64
HG_SUB = 16
ATTN_SCALE = HEAD_DIM ** -0.5
HG_SCALE = HEAD_DIM ** -0.5
EPS = 1e-6
EXP2_SCALE = ATTN_SCALE * 1.4426950408889634
MASK_BIAS = 2.0 ** 40

ZA_COLS = NSA_WIDTH + 3 * KV_COLS
ZH_COLS = 4 * HG_WIDTH
ZG_COLS = 128
PROJ_TN = 512
COL_CMP = NSA_WIDTH
COL_SLC = NSA_WIDTH + KV_COLS
COL_WIN = NSA_WIDTH + 2 * KV_COLS

VMEM_LIMIT = 56 * 1024 * 1024


def _cparams(sem):
    return pltpu.CompilerParams(dimension_semantics=sem, vmem_limit_bytes=VMEM_LIMIT)


def _rms(x, g):
    return x * lax.rsqrt(jnp.mean(x * x, axis=-1, keepdims=True) + EPS) * g


def _dot(a, b):
    return jnp.dot(a, b, preferred_element_type=F32)


def _dot_nt(a, b):
    return lax.dot_general(a, b, (((1,), (1,)), ((), ())), preferred_element_type=F32)


def _dot_tn(a, b):
    return lax.dot_general(a, b, (((0,), (0,)), ((), ())), preferred_element_type=F32)


def _split3(x):
    hi = x.astype(BF16)
    r1 = x - hi.astype(F32)
    mid = r1.astype(BF16)
    lo = (r1 - mid.astype(F32)).astype(BF16)
    return hi, mid, lo


def _ffn_kernel(x_ref, g_ref, wg_ref, wu_ref, wd_ref, pg_ref, *rest, emit_x):
    if emit_x:
        xo_ref, no_ref, h_sc, acc_sc = rest
    else:
        no_ref, h_sc, acc_sc = rest
    j = pl.program_id(1)

    @pl.when(j == 0)
    def _():
        h_sc[...] = _rms(x_ref[...], g_ref[...]).astype(BF16)
        acc_sc[...] = jnp.zeros_like(acc_sc)

    h = h_sc[...]
    a = _dot(h, wg_ref[...])
    u = _dot(h, wu_ref[...])
    act = (a * jax.nn.sigmoid(a) * u).astype(BF16)
    acc_sc[...] += _dot(act, wd_ref[...])

    @pl.when(j == pl.num_programs(1) - 1)
    def _():
        xn = x_ref[...] + 0.5 * acc_sc[...]
        if emit_x:
            xo_ref[...] = xn
        no_ref[...] = _rms(xn, pg_ref[...]).astype(no_ref.dtype)


def _ffn(x, g, wg, wu, wd, post_g, *, emit_x, norm_dtype, tf=512):
    n, d = x.shape
    fp = wg.shape[1]
    tm = min(n, 512)
    assert n % tm == 0 and fp % tf == 0
    row = pl.BlockSpec((tm, d), lambda i, j: (i, 0))
    vec = pl.BlockSpec((1, d), lambda i, j: (0, 0))
    out_shape = [jax.ShapeDtypeStruct((n, d), norm_dtype)]
    out_specs = [row]
    if emit_x:
        out_shape = [jax.ShapeDtypeStruct((n, d), F32)] + out_shape
        out_specs = [row] + out_specs
    return pl.pallas_call(
        functools.partial(_ffn_kernel, emit_x=emit_x),
        grid=(n // tm, fp // tf),
        in_specs=[row, vec,
                  pl.BlockSpec((d, tf), lambda i, j: (0, j)),
                  pl.BlockSpec((d, tf), lambda i, j: (0, j)),
                  pl.BlockSpec((tf, d), lambda i, j: (j, 0)),
                  vec],
        out_specs=out_specs,
        out_shape=out_shape,
        scratch_shapes=[pltpu.VMEM((tm, d), BF16), pltpu.VMEM((tm, d), F32)],
        compiler_params=_cparams(("parallel", "arbitrary")),
        name="ffn_half_step",
    )(x, g.reshape(1, d), wg, wu, wd, post_g.reshape(1, d))


def _inproj_kernel(h_ref, w_ref, za_ref, za16_ref, zh_ref, zg_ref, *, na, nh):
    j = pl.program_id(1)
    r = _dot(h_ref[...], w_ref[...])

    @pl.when(j < na)
    def _():
        za_ref[...] = r
        za16_ref[...] = r.astype(BF16)

    @pl.when((j >= na) & (j < na + nh))
    def _():
        zh_ref[...] = r

    @pl.when(j == na + nh)
    def _():
        zg_ref[...] = r[:, :ZG_COLS]


def _inproj(h, w):
    n, d = h.shape
    tn = PROJ_TN
    na, nh = ZA_COLS // tn, ZH_COLS // tn
    tm = min(n, 1024)
    assert n % tm == 0 and w.shape[1] == (na + nh + 1) * tn
    return pl.pallas_call(
        functools.partial(_inproj_kernel, na=na, nh=nh),
        grid=(n // tm, na + nh + 1),
        in_specs=[pl.BlockSpec((tm, d), lambda i, j: (i, 0)),
                  pl.BlockSpec((d, tn), lambda i, j: (0, j))],
        out_specs=[pl.BlockSpec((tm, tn), lambda i, j: (i, jnp.minimum(j, na - 1))),
                   pl.BlockSpec((tm, tn), lambda i, j: (i, jnp.minimum(j, na - 1))),
                   pl.BlockSpec((tm, tn), lambda i, j: (i, jnp.clip(j - na, 0, nh - 1))),
                   pl.BlockSpec((tm, ZG_COLS), lambda i, j: (i, 0))],
        out_shape=[jax.ShapeDtypeStruct((n, ZA_COLS), F32),
                   jax.ShapeDtypeStruct((n, ZA_COLS), BF16),
                   jax.ShapeDtypeStruct((n, ZH_COLS), F32),
                   jax.ShapeDtypeStruct((n, ZG_COLS), F32)],
        compiler_params=_cparams(("parallel", "arbitrary")),
        name="in_projection",
    )(h, w)


def _outproj_kernel(x_ref, a_ref, b_ref, w_ref, o_ref):
    half = a_ref.shape[1]
    o_ref[...] = (x_ref[...]
                  + _dot(a_ref[...].astype(BF16), w_ref[0:half, :])
                  + _dot(b_ref[...].astype(BF16), w_ref[half:, :]))


def _outproj(x, a, b, w):
    n, d = x.shape
    half = a.shape[1]
    tm = min(n, 512)
    assert n % tm == 0
    return pl.pallas_call(
        _outproj_kernel,
        grid=(n // tm,),
        in_specs=[pl.BlockSpec((tm, d), lambda i: (i, 0)),
                  pl.BlockSpec((tm, half), lambda i: (i, 0)),
                  pl.BlockSpec((tm, half), lambda i: (i, 0)),
                  pl.BlockSpec((2 * half, d), lambda i: (0, 0))],
        out_specs=pl.BlockSpec((tm, d), lambda i: (i, 0)),
        out_shape=jax.ShapeDtypeStruct((n, d), F32),
        compiler_params=_cparams(("parallel",)),
        name="out_projection",
    )(x, a, b, w)


def _cmp_bias_kernel(pe_ref, w1_ref, b1_ref, o_ref):
    for c in range(2):
        pe8 = jnp.broadcast_to(pe_ref[c], (8, pe_ref.shape[2])).astype(BF16)
        o_ref[c] = _dot(pe8, w1_ref[c])[0:1, :] + b1_ref[c]


def _cmp_bias(pe, w1, b1):
    flat = CMP_BLOCK * HEAD_DIM
    return pl.pallas_call(
        _cmp_bias_kernel,
        out_shape=jax.ShapeDtypeStruct((2, 1, HEAD_DIM), F32),
        name="cmp_bias",
    )(pe.reshape(2, 1, flat), w1.reshape(2, flat, HEAD_DIM).astype(BF16), b1.reshape(2, 1, HEAD_DIM))


def _cmp_accumulate(get_rows, w1r_ref, hp_sc, seg0, nss):
    for c in range(2):
        acc = None
        for r in range(CMP_STRIDE):
            xr = jnp.concatenate([get_rows(r, 2 * c + k) for k in range(NSA_KV)], axis=0).astype(BF16)
            d = _dot(xr, w1r_ref[c, r])
            acc = d if acc is None else acc + d
        for k in range(NSA_KV):
            hp_sc[2 * c + k, pl.ds(seg0, nss), :] = acc[k * nss:(k + 1) * nss]


def _cmp_finish(hp_sc, bias_ref, w2_ref, o_ref):
    nseg = hp_sc.shape[1]
    last = lax.broadcasted_iota(jnp.int32, (nseg, 1), 0) == nseg - 1
    for ck in range(2 * NSA_KV):
        c = ck // NSA_KV
        first = hp_sc[ck, :, 0:HEAD_DIM]
        second = pltpu.roll(hp_sc[ck, :, HEAD_DIM:2 * HEAD_DIM], nseg - 1, 0)
        pre = first + second + bias_ref[c]
        out = _dot((pre * jax.nn.sigmoid(pre)).astype(BF16), w2_ref[c])
        o_ref[0, ck] = jnp.where(last, 0.0, out).astype(o_ref.dtype)


def _cmp_prompt_kernel(x0_ref, x1_ref, x2_ref, x3_ref, w1r_ref, bias_ref, w2_ref, o_ref, hp_sc, *, nss):
    j = pl.program_id(1)
    x_refs = (x0_ref, x1_ref, x2_ref, x3_ref)

    def get_rows(r, ck):
        return x_refs[ck][pl.ds(r, nss, stride=CMP_STRIDE), :]

    _cmp_accumulate(get_rows, w1r_ref, hp_sc, pl.multiple_of(j * nss, nss), nss)

    @pl.when(j == pl.num_programs(1) - 1)
    def _():
        _cmp_finish(hp_sc, bias_ref, w2_ref, o_ref)


def _cmp_weight_specs(grid_rank):
    zeros = lambda n: (lambda *a: (0,) * n)
    return [pl.BlockSpec((2, CMP_STRIDE, HEAD_DIM, 2 * HEAD_DIM), zeros(4)),
            pl.BlockSpec((2, 1, HEAD_DIM), zeros(3)),
            pl.BlockSpec((2, HEAD_DIM, HEAD_DIM), zeros(3))]


def _cmp_prompt(za, b, t, w1r, bias, w2):
    rc = min(t, 2048)
    nss = rc // CMP_STRIDE
    nseg = t // CMP_STRIDE
    steps = t // rc
    assert t % rc == 0
    return pl.pallas_call(
        functools.partial(_cmp_prompt_kernel, nss=nss),
        grid=(b, steps),
        in_specs=[pl.BlockSpec((rc, HEAD_DIM), functools.partial(lambda i, j, ck: (i * steps + j, COL_CMP // HEAD_DIM + ck), ck=ck))
                  for ck in range(2 * NSA_KV)] + _cmp_weight_specs(2),
        out_specs=pl.BlockSpec((1, 2 * NSA_KV, nseg, HEAD_DIM), lambda i, j: (i, 0, 0, 0)),
        out_shape=jax.ShapeDtypeStruct((b, 2 * NSA_KV, nseg, HEAD_DIM), BF16),
        scratch_shapes=[pltpu.VMEM((2 * NSA_KV, nseg, 2 * HEAD_DIM), F32)],
        compiler_params=_cparams(("parallel", "arbitrary")),
        name="cmp_mlp_prompt",
    )(za, za, za, za, w1r, bias, w2)


CMP_PAGES = 16


KV_GROUPS = 2 * NSA_KV
PAGE_ROWS = PAGE_SIZE * KV_GROUPS


def _cmp_paged_kernel(pt_ref, *refs):
    pages = refs[:CMP_PAGES]
    perm_ref, w1p_ref, bias_ref, w2_ref, o_ref, hp_sc, x_sc = refs[CMP_PAGES:]
    j = pl.program_id(1)
    spp = PAGE_SIZE // CMP_STRIDE
    nss = CMP_PAGES * spp
    seg0 = pl.multiple_of(j * nss, nss)
    for c in range(2):
        for g, pg in enumerate(pages):
            xc = jnp.concatenate([pg[pl.ds(NSA_KV * c + k, PAGE_SIZE, stride=KV_GROUPS), :] for k in range(NSA_KV)],
                                 axis=1).astype(BF16)
            xp = _dot(perm_ref[...], xc)
            for r in range(CMP_STRIDE):
                for k in range(NSA_KV):
                    x_sc[r, k, g * spp:(g + 1) * spp, :] = xp[r * spp:(r + 1) * spp, k * HEAD_DIM:(k + 1) * HEAD_DIM]
        acc = None
        for rp in range(CMP_STRIDE // 2):
            lhs = jnp.concatenate([x_sc[2 * rp].reshape(NSA_KV * nss, HEAD_DIM),
                                   x_sc[2 * rp + 1].reshape(NSA_KV * nss, HEAD_DIM)], axis=1).astype(BF16)
            d = _dot(lhs, w1p_ref[c, rp])
            acc = d if acc is None else acc + d
        for k in range(NSA_KV):
            hp_sc[2 * c + k, pl.ds(seg0, nss), :] = acc[k * nss:(k + 1) * nss]

    @pl.when(j == pl.num_programs(1) - 1)
    def _():
        _cmp_finish(hp_sc, bias_ref, w2_ref, o_ref)


def _page_specs(n, page0):
    def spec(g):
        return pl.BlockSpec((PAGE_ROWS, HEAD_DIM), lambda i, j, pt: (page0 + pt[i, j * n + g], 0))
    return [spec(g) for g in range(n)]


def _cmp_paged(cache, page0, page_table, w1r, bias, w2):
    db, n_pages = page_table.shape
    assert n_pages % CMP_PAGES == 0 and PAGE_SIZE == HEAD_DIM
    nseg = n_pages * PAGE_SIZE // CMP_STRIDE
    spp = PAGE_SIZE // CMP_STRIDE
    nss = CMP_PAGES * spp
    i = jnp.arange(PAGE_SIZE)
    perm = (i[None, :] == ((i % spp) * CMP_STRIDE + i // spp)[:, None]).astype(BF16)
    w1p = w1r.reshape(2, CMP_STRIDE // 2, 2 * HEAD_DIM, 2 * HEAD_DIM)
    zeros = lambda n: (lambda i, j, pt: (0,) * n)
    grid_spec = pltpu.PrefetchScalarGridSpec(
        num_scalar_prefetch=1,
        grid=(db, n_pages // CMP_PAGES),
        in_specs=_page_specs(CMP_PAGES, page0) + [
            pl.BlockSpec((PAGE_SIZE, PAGE_SIZE), zeros(2)),
            pl.BlockSpec((2, CMP_STRIDE // 2, 2 * HEAD_DIM, 2 * HEAD_DIM), zeros(4)),
            pl.BlockSpec((2, 1, HEAD_DIM), zeros(3)),
            pl.BlockSpec((2, HEAD_DIM, HEAD_DIM), zeros(3))],
        out_specs=pl.BlockSpec((1, 2 * NSA_KV, nseg, HEAD_DIM), lambda i, j, pt: (i, 0, 0, 0)),
        scratch_shapes=[pltpu.VMEM((2 * NSA_KV, nseg, 2 * HEAD_DIM), F32),
                        pltpu.VMEM((CMP_STRIDE, NSA_KV, nss, HEAD_DIM), F32)])
    return pl.pallas_call(
        _cmp_paged_kernel,
        grid_spec=grid_spec,
        out_shape=jax.ShapeDtypeStruct((db, 2 * NSA_KV, nseg, HEAD_DIM), BF16),
        compiler_params=_cparams(("parallel", "arbitrary")),
        name="cmp_mlp_paged",
    )(page_table, *([cache] * CMP_PAGES), perm, w1p, bias, w2)


def _softmax0(s):
    m = jnp.max(s, axis=0, keepdims=True)
    m = jnp.where(m > NEG_INF, m, 0.0)
    e = jnp.exp2((s - m) * EXP2_SCALE)
    l = jnp.sum(e, axis=0, keepdims=True)
    return e * (1.0 / jnp.maximum(l, 1e-30))


def _cmp_attn_t(ck, cv, q, tpos):
    ncp = ck.shape[0]
    s = _dot_nt(ck, q)
    cend = lax.broadcasted_iota(jnp.int32, (ncp, 1), 0) * CMP_STRIDE + (CMP_BLOCK - 1)
    p = _softmax0(jnp.where(cend <= tpos, s, NEG_INF))
    return p, _dot_tn(cv, p.astype(BF16))


def _sel_importance_t(msel, u):
    hi, mid, lo = _split3(u)
    return _dot(msel, hi) + _dot(msel, mid) + _dot(msel, lo)


def _topk_t(pslc, tpos, nsel):
    nsp = pslc.shape[0]
    jio = lax.broadcasted_iota(jnp.int32, pslc.shape, 0)
    jt = lax.shift_right_logical(tpos, SEL_SHIFT)
    forced = (jio == 0) | (jio == jt) | (jio == jt - 1)
    score = jnp.where(forced, jnp.inf, jnp.where(jio <= jt, pslc, NEG_INF))
    sel = jnp.zeros(pslc.shape, F32)
    for _ in range(nsel):
        m = jnp.max(score, axis=0, keepdims=True)
        idx = jnp.min(jnp.where(score == m, jio, nsp), axis=0, keepdims=True)
        hit = jio == idx
        score = jnp.where(hit, NEG_INF, score)
        sel = jnp.where(hit, 1.0, sel)
    return sel


def _nsa_prompt_kernel(q_ref, ks_ref, vs_ref, kw_ref, vw_ref, eb_ref, ck_ref, cv_ref, zg_ref, msel_ref, gain_ref,
                       o_ref, gate_sc, *, tq, tk, nsel):
    k = pl.program_id(1)
    i = pl.program_id(2)
    t0 = i * tq
    r = NSA_GROUP * tq
    q = q_ref[...]
    qr = jnp.concatenate([q[:, g * HEAD_DIM:(g + 1) * HEAD_DIM] for g in range(NSA_GROUP)], axis=0)
    tpos1 = t0 + lax.broadcasted_iota(jnp.int32, (1, tq), 1)
    tpos = jnp.concatenate([tpos1] * NSA_GROUP, axis=1)

    p, o_cmp = _cmp_attn_t(ck_ref[0, 0], cv_ref[0, 0], qr, tpos)
    imp = p[:, 0:tq]
    for g in range(1, NSA_GROUP):
        imp = imp + p[:, g * tq:(g + 1) * tq]
    sel = _topk_t(_sel_importance_t(msel_ref[...], imp), tpos1, nsel)

    selm1 = (sel - 1.0).T.astype(BF16)
    q_aug = jnp.concatenate([qr, jnp.concatenate([selm1] * NSA_GROUP, axis=0)], axis=1)
    ones8 = jnp.ones((8, tk), BF16)

    def scores(kt):
        off = pl.multiple_of(kt * tk, tk)
        k_aug = jnp.concatenate([ks_ref[pl.ds(off, tk), :], eb_ref[pl.ds(off, tk), :]], axis=1)
        return _dot_nt(k_aug, q_aug)

    def absorb(kt, s, stats):
        m, l, acc = stats
        off = pl.multiple_of(kt * tk, tk)
        m_new = jnp.maximum(m, jnp.max(s, axis=0, keepdims=True))
        alpha = jnp.exp2((m - m_new) * EXP2_SCALE)
        e = jnp.exp2((s - m_new) * EXP2_SCALE).astype(BF16)
        l = alpha * l + _dot(ones8, e)[0:1, :]
        acc = alpha * acc + _dot_tn(vs_ref[pl.ds(off, tk), :], e)
        return m_new, l, acc

    def step(kt, carry):
        s, stats = carry
        s_next = scores(kt + 1)
        return s_next, absorb(kt, s, stats)

    n_full = t0 // tk
    init = (jnp.full((1, r), NEG_INF, F32), jnp.zeros((1, r), F32), jnp.zeros((HEAD_DIM, r), F32))
    s, stats = lax.fori_loop(0, n_full, step, (scores(0), init))
    diag_off = n_full * tk
    s = jnp.where(diag_off + lax.broadcasted_iota(jnp.int32, (tk, 1), 0) <= tpos, s, NEG_INF)
    _, l, acc = absorb(n_full, s, stats)
    o_slc = acc * (1.0 / jnp.maximum(l, 1e-30))

    wlen = WINDOW + tq
    ws = pl.multiple_of(jnp.maximum(t0 - WINDOW, 0), tq)
    d = tpos - (ws + lax.broadcasted_iota(jnp.int32, (wlen, 1), 0))
    s = _dot_nt(kw_ref[pl.ds(ws, wlen), :], qr)
    pw = _softmax0(jnp.where((d >= 0) & (d <= WINDOW), s, NEG_INF))
    o_win = _dot_tn(vw_ref[pl.ds(ws, wlen), :], pw.astype(BF16))

    gate_sc[...] = jax.nn.sigmoid(zg_ref[...].T[0:gate_sc.shape[0], :])

    def gate_row(branch):
        return jnp.concatenate(
            [gate_sc[pl.ds(branch * NSA_HEADS + k * NSA_GROUP + g, 1), :] for g in range(NSA_GROUP)], axis=1)

    o = gate_row(0) * o_cmp + gate_row(1) * o_slc + gate_row(2) * o_win
    o = o * lax.rsqrt(jnp.mean(o * o, axis=0, keepdims=True) + EPS)
    for g in range(NSA_GROUP):
        og = o[:, g * tq:(g + 1) * tq] * gain_ref[k * NSA_GROUP + g]
        o_ref[:, g * HEAD_DIM:(g + 1) * HEAD_DIM] = og.T.astype(o_ref.dtype)


def _nsa_prompt(za16, ckv, zg, msel, gain_b, b, t):
    tq = 128
    tk = min(t, 512)
    nq = t // tq
    ncp = ckv.shape[2]
    nsp = msel.shape[0]
    qw = NSA_GROUP * HEAD_DIM
    assert t % tk == 0 and tk % tq == 0 and t >= WINDOW + tq and nsp == 128
    col = lambda base, kk: (base // HEAD_DIM) + kk
    block_bias = jnp.where(jnp.arange(t)[:, None] // SEL_BLOCK == jnp.arange(nsp)[None, :], MASK_BIAS, 0.0).astype(BF16)

    def seq_spec(base):
        return pl.BlockSpec((t, HEAD_DIM), lambda bi, k, i: (bi, col(base, k)))

    return pl.pallas_call(
        functools.partial(_nsa_prompt_kernel, tq=tq, tk=tk, nsel=min(N_SEL, t // SEL_BLOCK)),
        grid=(b, NSA_KV, nq),
        in_specs=[pl.BlockSpec((tq, qw), lambda bi, k, i: (bi * nq + i, k)),
                  seq_spec(COL_SLC), seq_spec(COL_SLC + NSA_KV * HEAD_DIM),
                  seq_spec(COL_WIN), seq_spec(COL_WIN + NSA_KV * HEAD_DIM),
                  pl.BlockSpec((t, nsp), lambda bi, k, i: (0, 0)),
                  pl.BlockSpec((1, 1, ncp, HEAD_DIM), lambda bi, k, i: (bi, k, 0, 0)),
                  pl.BlockSpec((1, 1, ncp, HEAD_DIM), lambda bi, k, i: (bi, NSA_KV + k, 0, 0)),
                  pl.BlockSpec((tq, ZG_COLS), lambda bi, k, i: (bi * nq + i, 0)),
                  pl.BlockSpec((nsp, ncp), lambda bi, k, i: (0, 0)),
                  pl.BlockSpec((NSA_HEADS, HEAD_DIM, tq), lambda bi, k, i: (0, 0, 0))],
        out_specs=pl.BlockSpec((tq, qw), lambda bi, k, i: (bi * nq + i, k)),
        out_shape=jax.ShapeDtypeStruct((b * t, NSA_WIDTH), BF16),
        scratch_shapes=[pltpu.VMEM((32, tq), F32)],
        compiler_params=_cparams(("parallel", "parallel", "arbitrary")),
        name="nsa_prompt",
    )(za16, za16, za16, za16, za16, block_bias, ckv, ckv, zg, msel, gain_b)


SLC_PAGES = 16


def _nsa_sample_kernel(pt_ref, *refs, ts, past_len):
    pages = refs[:SLC_PAGES]
    (z_ref, zg_ref, ckv_ref, win_ref, msel_ref, e_ref, gain_ref,
     o_ref, sel_sc, m_sc, l_sc, acc_sc, ocmp_sc) = refs[SLC_PAGES:]
    j = pl.program_id(1)
    rows = NSA_GROUP * ts
    lanes = 128
    step_keys = SLC_PAGES * PAGE_SIZE
    step_blocks = step_keys // SEL_BLOCK
    z = z_ref[...]

    def q_rows(k):
        return jnp.concatenate([z[:, (k * NSA_GROUP + g) * HEAD_DIM:(k * NSA_GROUP + g + 1) * HEAD_DIM]
                                for g in range(NSA_GROUP)], axis=0).astype(BF16)

    @pl.when(j == 0)
    def _():
        lane = lax.broadcasted_iota(jnp.int32, (1, lanes), 1)
        tpos = past_len + jnp.bitwise_and(lane, ts - 1)
        s = None
        for k in range(NSA_KV):
            pads = (k * rows, lanes - (k + 1) * rows)
            parts = [jnp.zeros((pads[0], HEAD_DIM), BF16), q_rows(k), jnp.zeros((pads[1], HEAD_DIM), BF16)]
            qp = jnp.concatenate([x for x in parts if x.shape[0] > 0], axis=0)
            sk = _dot_nt(ckv_ref[0, k], qp)
            s = sk if s is None else s + sk
        ncp = s.shape[0]
        cend = lax.broadcasted_iota(jnp.int32, (ncp, 1), 0) * CMP_STRIDE + (CMP_BLOCK - 1)
        p = _softmax0(jnp.where(cend <= tpos, s, NEG_INF))
        p16 = p.astype(BF16)
        for k in range(NSA_KV):
            ocmp_sc[k] = _dot_tn(ckv_ref[0, NSA_KV + k], p16).T[k * rows:(k + 1) * rows, :]
            m_sc[k] = jnp.full((rows, 1), NEG_INF, F32)
            l_sc[k] = jnp.zeros((rows, 1), F32)
            acc_sc[k] = jnp.zeros((rows, HEAD_DIM), F32)
        u = _sel_importance_t(msel_ref[...], p)
        pslc = u
        for g in range(1, NSA_GROUP):
            pslc = pslc + pltpu.roll(u, lanes - g * ts, 1)
        sel_sc[...] = _topk_t(pslc, tpos, N_SEL)

    def online(k, s, v):
        m_old = m_sc[k]
        m_new = jnp.maximum(m_old, jnp.max(s, axis=1, keepdims=True))
        alpha = jnp.exp2((m_old - m_new) * EXP2_SCALE)
        e = jnp.exp2((s - m_new) * EXP2_SCALE)
        l_sc[k] = alpha * l_sc[k] + jnp.sum(e, axis=1, keepdims=True)
        acc_sc[k] = alpha * acc_sc[k] + _dot(e.astype(BF16), v)
        m_sc[k] = m_new

    def kv_rows(ref, ck, n):
        return ref[pl.ds(ck, n, stride=KV_GROUPS), :].astype(BF16)

    selj = sel_sc[pl.ds(pl.multiple_of(j * step_blocks, step_blocks), step_blocks), :]
    mask_all = _dot_tn(selj.astype(BF16), e_ref[...])
    for k in range(NSA_KV):
        kt = jnp.concatenate([kv_rows(pg, k, PAGE_SIZE) for pg in pages], axis=0)
        vt = jnp.concatenate([kv_rows(pg, NSA_KV + k, PAGE_SIZE) for pg in pages], axis=0)
        mask = jnp.concatenate([mask_all[k * rows:k * rows + ts]] * NSA_GROUP, axis=0) > 0.5
        online(k, jnp.where(mask, _dot_nt(q_rows(k), kt), NEG_INF), vt)

    @pl.when(j == pl.num_programs(1) - 1)
    def _():
        row_t = jnp.bitwise_and(lax.broadcasted_iota(jnp.int32, (rows, 1), 0), ts - 1)
        new_i = lax.broadcasted_iota(jnp.int32, (1, 2 * ts), 1)
        new_ok = (new_i < ts) & (new_i <= row_t)
        pad = jnp.zeros((ts, HEAD_DIM), F32)

        def new_rows(col):
            return jnp.concatenate([z[:, col:col + HEAD_DIM], pad], axis=0).astype(BF16)

        wlen = win_ref.shape[0] // KV_GROUPS
        wpos = past_len - wlen + lax.broadcasted_iota(jnp.int32, (1, wlen), 1)
        dw = (past_len + row_t) - wpos
        win_ok = (dw >= 0) & (dw <= WINDOW) & (wpos >= 0)
        zg = zg_ref[...]
        for k in range(NSA_KV):
            q = q_rows(k)
            s_new = _dot_nt(q, new_rows(COL_SLC + k * HEAD_DIM))
            online(k, jnp.where(new_ok, s_new, NEG_INF), new_rows(COL_SLC + (NSA_KV + k) * HEAD_DIM))
            o_slc = acc_sc[k] * (1.0 / jnp.maximum(l_sc[k], 1e-30))
            kw = kv_rows(win_ref, k, wlen)
            vw = kv_rows(win_ref, NSA_KV + k, wlen)
            s1 = jnp.where(win_ok, _dot_nt(q, kw), NEG_INF)
            s2 = jnp.where(new_ok, _dot_nt(q, new_rows(COL_WIN + k * HEAD_DIM)), NEG_INF)
            mw = jnp.maximum(jnp.max(s1, axis=1, keepdims=True), jnp.max(s2, axis=1, keepdims=True))
            e1 = jnp.exp2((s1 - mw) * EXP2_SCALE)
            e2 = jnp.exp2((s2 - mw) * EXP2_SCALE)
            lw = jnp.sum(e1, axis=1, keepdims=True) + jnp.sum(e2, axis=1, keepdims=True)
            o_win = (_dot(e1.astype(BF16), vw)
                     + _dot(e2.astype(BF16), new_rows(COL_WIN + (NSA_KV + k) * HEAD_DIM))) * (1.0 / jnp.maximum(lw, 1e-30))

            def gate(branch):
                c0 = branch * NSA_HEADS + k * NSA_GROUP
                return jnp.concatenate([jax.nn.sigmoid(zg[:, c0 + g:c0 + g + 1]) for g in range(NSA_GROUP)], axis=0)

            o = gate(0) * ocmp_sc[k] + gate(1) * o_slc + gate(2) * o_win
            o = o * lax.rsqrt(jnp.mean(o * o, axis=1, keepdims=True) + EPS)
            for g in range(NSA_GROUP):
                h = k * NSA_GROUP + g
                o_ref[:, h * HEAD_DIM:(h + 1) * HEAD_DIM] = o[g * ts:(g + 1) * ts, :] * gain_ref[h:h + 1, :]


def _nsa_sample(za, zg, ckv, cache_slc, page0, win, page_table, msel, e_blocks, gain, ts):
    db, n_pages = page_table.shape
    past_len = n_pages * PAGE_SIZE
    ncp = ckv.shape[2]
    nsp = msel.shape[0]
    wrows = win.shape[0] // db
    rows = NSA_GROUP * ts
    step_keys = SLC_PAGES * PAGE_SIZE
    assert n_pages % SLC_PAGES == 0 and ts & (ts - 1) == 0 and ts % 8 == 0 and rows <= 128
    assert past_len % SEL_BLOCK == 0 and ts <= SEL_BLOCK and ts < CMP_STRIDE
    zeros = lambda n: (lambda i, j, pt: (0,) * n)
    grid_spec = pltpu.PrefetchScalarGridSpec(
        num_scalar_prefetch=1,
        grid=(db, n_pages // SLC_PAGES),
        in_specs=_page_specs(SLC_PAGES, page0) + [
            pl.BlockSpec((ts, ZA_COLS), lambda i, j, pt: (i, 0)),
            pl.BlockSpec((ts, ZG_COLS), lambda i, j, pt: (i, 0)),
            pl.BlockSpec((1, 2 * NSA_KV, ncp, HEAD_DIM), lambda i, j, pt: (i, 0, 0, 0)),
            pl.BlockSpec((wrows, HEAD_DIM), lambda i, j, pt: (i, 0)),
            pl.BlockSpec((nsp, ncp), zeros(2)),
            pl.BlockSpec((step_keys // SEL_BLOCK, step_keys), zeros(2)),
            pl.BlockSpec((NSA_HEADS, HEAD_DIM), zeros(2))],
        out_specs=pl.BlockSpec((ts, NSA_WIDTH), lambda i, j, pt: (i, 0)),
        scratch_shapes=[pltpu.VMEM((nsp, 128), F32),
                        pltpu.VMEM((NSA_KV, rows, 1), F32),
                        pltpu.VMEM((NSA_KV, rows, 1), F32),
                        pltpu.VMEM((NSA_KV, rows, HEAD_DIM), F32),
                        pltpu.VMEM((NSA_KV, rows, HEAD_DIM), F32)])
    return pl.pallas_call(
        functools.partial(_nsa_sample_kernel, ts=ts, past_len=past_len),
        grid_spec=grid_spec,
        out_shape=jax.ShapeDtypeStruct((db * ts, NSA_WIDTH), F32),
        compiler_params=_cparams(("parallel", "arbitrary")),
        name="nsa_sample",
    )(page_table, *([cache_slc] * SLC_PAGES), za, zg, ckv, win, msel, e_blocks, gain)


def _cumsum0(x):
    n = x.shape[0]
    row = lax.broadcasted_iota(jnp.int32, (n, 1), 0)
    sh = 1
    while sh < n:
        x = x + jnp.where(row >= sh, pltpu.roll(x, sh, 0), 0.0)
        sh *= 2
    return x


HG_PAR = 4


def _hgrn_head_chunk(hq, hf, v, gt, lb, gain, st, *, c, sb):
    fg = lb + (1.0 - lb) * jax.nn.sigmoid(hf)
    kk = 1.0 - fg
    bc = _cumsum0(jnp.log(fg))
    qh = hq * jax.nn.sigmoid(hq) * HG_SCALE
    o_inter = _dot_nt((qh * jnp.exp(bc)).astype(BF16), st.astype(BF16))
    v16 = v.astype(BF16)
    row8 = lax.broadcasted_iota(jnp.int32, (8, 1), 0)
    pieces = []
    for si in range(c // sb):
        lo = si * sb
        od = [jnp.zeros((8, HEAD_DIM), F32) for _ in range(sb // 8)]
        for s in range(sb):
            ks, vs_, bs = kk[lo + s:lo + s + 1], v[lo + s:lo + s + 1], bc[lo + s:lo + s + 1]
            for p in range(s // 8, sb // 8):
                r0 = lo + 8 * p
                w = jnp.exp(jnp.where(row8 + 8 * p >= s, bc[r0:r0 + 8] - bs, NEG_INF))
                a = jnp.sum(qh[r0:r0 + 8] * ks * w, axis=1, keepdims=True)
                od[p] = od[p] + a * vs_
        od = od[0] if len(od) == 1 else jnp.concatenate(od, axis=0)
        if si > 0:
            bcr = bc[lo - 1:lo]
            qs = (qh[lo:lo + sb] * jnp.exp(bc[lo:lo + sb] - bcr)).astype(BF16)
            kp = (kk[0:lo] * jnp.exp(bcr - bc[0:lo])).astype(BF16)
            od = od + _dot(_dot_nt(qs, kp).astype(BF16), v16[0:lo])
        pieces.append(od)
    o = o_inter + (pieces[0] if len(pieces) == 1 else jnp.concatenate(pieces, axis=0))
    bl = bc[c - 1:c]
    st_new = st * jnp.exp(bl) + _dot_tn(v16, (kk * jnp.exp(bl - bc)).astype(BF16))
    return _rms(o, gain) * (gt * jax.nn.sigmoid(gt)), st_new


def _hgrn_kernel(q_ref, f_ref, i_ref, g_ref, lbl_ref, on_ref, s0_ref, o_ref, so_ref, st_sc, *, c, sb, layer):
    j = pl.program_id(2)
    nh = st_sc.shape[0]

    @pl.when(j == 0)
    def _():
        for h in range(nh):
            st_sc[h] = s0_ref[0, h].T

    lbl = lbl_ref[...]
    e = jnp.exp(lbl - jnp.max(lbl, axis=0, keepdims=True))
    lb = jnp.sum(e[0:layer + 1], axis=0, keepdims=True) / jnp.sum(e, axis=0, keepdims=True)
    gain = on_ref[...]

    def chunk(ci, carry):
        rows = pl.ds(pl.multiple_of(ci * c, c), c)
        for h in range(nh):
            cols = slice(h * HEAD_DIM, (h + 1) * HEAD_DIM)
            y, st_new = _hgrn_head_chunk(q_ref[rows, cols], f_ref[rows, cols], i_ref[rows, cols], g_ref[rows, cols],
                                         lb[:, cols], gain[:, cols], st_sc[h], c=c, sb=sb)
            st_sc[h] = st_new
            o_ref[rows, cols] = y.astype(o_ref.dtype)
        return carry

    lax.fori_loop(0, q_ref.shape[0] // c, chunk, 0)

    @pl.when(j == pl.num_programs(2) - 1)
    def _():
        for h in range(nh):
            so_ref[0, h] = st_sc[h].T


def _hgrn(zh, lb_logits, out_norm, s0, b, t, layer, out_dtype):
    c = HG_CHUNK if t % HG_CHUNK == 0 else t
    sb = min(HG_SUB, c)
    tc = min(t, 512)
    nt = t // tc
    hw = HG_PAR * HEAD_DIM
    ng = HG_HEADS // HG_PAR
    assert t % tc == 0 and tc % c == 0 and c % sb == 0 and sb % 8 == 0 and HG_HEADS % HG_PAR == 0

    def piece(p):
        return pl.BlockSpec((tc, hw), lambda bi, h, j: (bi * nt + j, p * ng + h))

    st_spec = pl.BlockSpec((1, HG_PAR, HEAD_DIM, HEAD_DIM), lambda bi, h, j: (bi, h, 0, 0))
    nl = lb_logits.shape[0]
    return pl.pallas_call(
        functools.partial(_hgrn_kernel, c=c, sb=sb, layer=layer),
        grid=(b, ng, nt),
        in_specs=[piece(0), piece(1), piece(2), piece(3),
                  pl.BlockSpec((nl, hw), lambda bi, h, j: (0, h)),
                  pl.BlockSpec((1, hw), lambda bi, h, j: (0, h)),
                  st_spec],
        out_specs=[pl.BlockSpec((tc, hw), lambda bi, h, j: (bi * nt + j, h)), st_spec],
        out_shape=[jax.ShapeDtypeStruct((b * t, HG_WIDTH), out_dtype),
                   jax.ShapeDtypeStruct((b, HG_HEADS, HEAD_DIM, HEAD_DIM), F32)],
        scratch_shapes=[pltpu.VMEM((HG_PAR, HEAD_DIM, HEAD_DIM), F32)],
        compiler_params=_cparams(("parallel", "parallel", "arbitrary")),
        name="hgrn2",
    )(zh, zh, zh, zh, lb_logits, out_norm.reshape(1, HG_WIDTH), s0)


def _sel_coverage(ns_pad, ns, ncp, nc):
    j = jnp.arange(ns_pad)[:, None]
    i = jnp.arange(ncp)[None, :]
    d = i - (SEL_RATIO * j - 1)
    w = jnp.zeros((ns_pad, ncp), F32)
    for dd, ww in enumerate(SEL_SPAN_W):
        w = jnp.where(d == dd, ww, w)
    return jnp.where((j < ns) & (i < nc), w, 0.0).astype(BF16)


def _pack_w_in(w):
    gates = w[:, ZA_COLS:ZA_COLS + 3 * NSA_HEADS]
    return jnp.concatenate([w[:, :ZA_COLS], w[:, ZA_COLS + 3 * NSA_HEADS:],
                            jnp.pad(gates, ((0, 0), (0, PROJ_TN - 3 * NSA_HEADS)))], axis=1).astype(BF16)


def _pack_ffn(wg, wu, wd, tf=512):
    f = wg.shape[1]
    fp = -(-f // tf) * tf
    padc = ((0, 0), (0, fp - f))
    return (jnp.pad(wg, padc).astype(BF16), jnp.pad(wu, padc).astype(BF16),
            jnp.pad(wd, ((0, fp - f), (0, 0))).astype(BF16))


def kernel(x_prompt, x_sample, cache_cmp_kv, cache_slc_kv, state_win_kv, state_hgrn, page_table, ffn1_norm, ffn1_w_gate, ffn1_w_up, ffn1_w_down, mix_norm, w_in, cmp_pe, cmp_w1, cmp_b1, cmp_w2, nsa_out_norm, hg_lb_logits, hg_out_norm, w_out, ffn2_norm, ffn2_w_gate, ffn2_w_up, ffn2_w_down, final_norm):
    depth = w_in.shape[0]
    assert depth == 1, "single-layer trunk"
    l = 0
    b, t, d = x_prompt.shape
    db, ts, _ = x_sample.shape
    n_pool = cache_cmp_kv.shape[1]
    n_pages = page_table.shape[1]
    past_len = n_pages * PAGE_SIZE
    assert state_win_kv.shape[2] == min(WINDOW, past_len) and t % SEL_BLOCK == 0

    f1 = _pack_ffn(ffn1_w_gate[l], ffn1_w_up[l], ffn1_w_down[l])
    f2 = _pack_ffn(ffn2_w_gate[l], ffn2_w_up[l], ffn2_w_down[l])
    w_in_p = _pack_w_in(w_in[l])
    w_out_p = w_out[l].astype(BF16)
    w1 = cmp_w1[l]
    w1r = (w1.reshape(2, CMP_BLOCK // CMP_STRIDE, CMP_STRIDE, HEAD_DIM, HEAD_DIM)
           .transpose(0, 2, 3, 1, 4).reshape(2, CMP_STRIDE, HEAD_DIM, 2 * HEAD_DIM).astype(BF16))
    w2 = cmp_w2[l].astype(BF16)
    cmp_bias = _cmp_bias(cmp_pe[l], w1, cmp_b1[l])
    gain = nsa_out_norm[l]

    def trunk_front(x):
        x1, h = _ffn(x, ffn1_norm[l], *f1, mix_norm[l], emit_x=True, norm_dtype=BF16)
        return (x1,) + tuple(_inproj(h, w_in_p))

    def trunk_back(x1, o_nsa, o_hg):
        x2 = _outproj(x1, o_nsa, o_hg, w_out_p)
        return _ffn(x2, ffn2_norm[l], *f2, final_norm, emit_x=False, norm_dtype=F32)[0]

    def kv_rows(za, col, lead):
        return za[:, col:col + KV_COLS].reshape(*lead, 2, NSA_KV, HEAD_DIM)

    x1, za, za16, zh, zg = trunk_front(x_prompt.reshape(b * t, d))
    ns = t // SEL_BLOCK
    ncp = t // CMP_STRIDE
    msel = _sel_coverage(128, ns, ncp, ncp - 1)
    ckv = _cmp_prompt(za, b, t, w1r, cmp_bias, w2)
    gain_b = jnp.broadcast_to(gain[:, :, None], (NSA_HEADS, HEAD_DIM, 128))
    o_nsa = _nsa_prompt(za16, ckv, zg, msel, gain_b, b, t)
    s0 = jnp.zeros((b, HG_HEADS, HEAD_DIM, HEAD_DIM), F32)
    o_hg, hg_p = _hgrn(zh, hg_lb_logits, hg_out_norm[l], s0, b, t, l, BF16)
    y_prompt = trunk_back(x1, o_nsa, o_hg).reshape(b, t, d)
    p_cmp = kv_rows(za, COL_CMP, (b, t))
    p_slc = kv_rows(za, COL_SLC, (b, t))
    p_win = kv_rows(za, COL_WIN, (b, t))[:, t - min(WINDOW, t):]

    x1s, zas, _, zhs, zgs = trunk_front(x_sample.reshape(db * ts, d))
    ncp_s = past_len // CMP_STRIDE
    ns_s = -(-(past_len + ts) // SEL_BLOCK)
    ns_pad = -(-ns_s // 8) * 8
    msel_s = _sel_coverage(ns_pad, ns_s, ncp_s, ncp_s - 1)
    step_keys = SLC_PAGES * PAGE_SIZE
    e_blocks = (jnp.arange(step_keys)[None, :] // SEL_BLOCK == jnp.arange(step_keys // SEL_BLOCK)[:, None]).astype(BF16)
    ckv_s = _cmp_paged(cache_cmp_kv.reshape(-1, HEAD_DIM), l * n_pool, page_table, w1r, cmp_bias, w2)
    wlen = state_win_kv.shape[2]
    win = state_win_kv[l].reshape(db * wlen * KV_GROUPS, HEAD_DIM)
    o_nsa_s = _nsa_sample(zas, zgs, ckv_s, cache_slc_kv.reshape(-1, HEAD_DIM), l * n_pool, win,
                          page_table, msel_s, e_blocks, gain, ts)
    o_hg_s, hg_s = _hgrn(zhs, hg_lb_logits, hg_out_norm[l], state_hgrn[l], db, ts, l, F32)
    y_sample = trunk_back(x1s, o_nsa_s, o_hg_s).reshape(db, ts, d)
    s_cmp = kv_rows(zas, COL_CMP, (db, ts))
    s_slc = kv_rows(zas, COL_SLC, (db, ts))
    s_win_new = kv_rows(zas, COL_WIN, (db, ts))
    s_win = jnp.concatenate([state_win_kv[l].astype(F32), s_win_new], axis=1)[:, ts:]

    stack = lambda a: a[None]
    return (y_prompt, y_sample, stack(p_cmp), stack(p_slc), stack(p_win), hg_p[None].astype(state_hgrn.dtype),
            stack(s_cmp), stack(s_slc), stack(s_win), hg_s[None].astype(state_hgrn.dtype))
```

```python
import functools

import jax
import jax.numpy as jnp
from jax import lax
from jax.experimental import pallas as pl
from jax.experimental.pallas import tpu as pltpu

F32 = jnp.float32
BF16 = jnp.bfloat16
NEG_INF = float("-inf")

HEAD_DIM = 128
NSA_KV = 2
NSA_GROUP = 4
NSA_HEADS = NSA_KV * NSA_GROUP
NSA_WIDTH = NSA_HEADS * HEAD_DIM
KV_COLS = 2 * NSA_KV * HEAD_DIM
HG_HEADS = 8
HG_WIDTH = HG_HEADS * HEAD_DIM
CMP_BLOCK = 32
CMP_STRIDE = 16
SEL_BLOCK = 64
SEL_SHIFT = 6
SEL_RATIO = SEL_BLOCK // CMP_STRIDE
N_SEL = 16
SEL_SPAN_W = (1.0, 2.0, 2.0, 2.0, 1.0)
WINDOW = 512
PAGE_SIZE = 128
HG_CHUNK = 64
HG_SUB = 16
ATTN_SCALE = HEAD_DIM ** -0.5
HG_SCALE = HEAD_DIM ** -0.5
EPS = 1e-6
EXP2_SCALE = ATTN_SCALE * 1.4426950408889634
MASK_BIAS = 2.0 ** 40

ZA_COLS = NSA_WIDTH + 3 * KV_COLS
ZH_COLS = 4 * HG_WIDTH
ZG_COLS = 128
PROJ_TN = 512
COL_CMP = NSA_WIDTH
COL_SLC = NSA_WIDTH + KV_COLS
COL_WIN = NSA_WIDTH + 2 * KV_COLS

VMEM_LIMIT = 56 * 1024 * 1024


def _cparams(sem):
    return pltpu.CompilerParams(dimension_semantics=sem, vmem_limit_bytes=VMEM_LIMIT)


def _rms(x, g):
    return x * lax.rsqrt(jnp.mean(x * x, axis=-1, keepdims=True) + EPS) * g


def _dot(a, b):
    return jnp.dot(a, b, preferred_element_type=F32)


def _dot_nt(a, b):
    return lax.dot_general(a, b, (((1,), (1,)), ((), ())), preferred_element_type=F32)


def _dot_tn(a, b):
    return lax.dot_general(a, b, (((0,), (0,)), ((), ())), preferred_element_type=F32)


def _split3(x):
    hi = x.astype(BF16)
    r1 = x - hi.astype(F32)
    mid = r1.astype(BF16)
    lo = (r1 - mid.astype(F32)).astype(BF16)
    return hi, mid, lo


def _ffn_kernel(x_ref, g_ref, wg_ref, wu_ref, wd_ref, pg_ref, *rest, emit_x):
    if emit_x:
        xo_ref, no_ref, h_sc, acc_sc = rest
    else:
        no_ref, h_sc, acc_sc = rest
    j = pl.program_id(1)

    @pl.when(j == 0)
    def _():
        h_sc[...] = _rms(x_ref[...], g_ref[...]).astype(BF16)
        acc_sc[...] = jnp.zeros_like(acc_sc)

    h = h_sc[...]
    a = _dot(h, wg_ref[...])
    u = _dot(h, wu_ref[...])
    act = (a * jax.nn.sigmoid(a) * u).astype(BF16)
    acc_sc[...] += _dot(act, wd_ref[...])

    @pl.when(j == pl.num_programs(1) - 1)
    def _():
        xn = x_ref[...] + 0.5 * acc_sc[...]
        if emit_x:
            xo_ref[...] = xn
        no_ref[...] = _rms(xn, pg_ref[...]).astype(no_ref.dtype)


def _ffn(x, g, wg, wu, wd, post_g, *, emit_x, norm_dtype, tf=512):
    n, d = x.shape
    fp = wg.shape[1]
    tm = min(n, 512)
    assert n % tm == 0 and fp % tf == 0
    row = pl.BlockSpec((tm, d), lambda i, j: (i, 0))
    vec = pl.BlockSpec((1, d), lambda i, j: (0, 0))
    out_shape = [jax.ShapeDtypeStruct((n, d), norm_dtype)]
    out_specs = [row]
    if emit_x:
        out_shape = [jax.ShapeDtypeStruct((n, d), F32)] + out_shape
        out_specs = [row] + out_specs
    return pl.pallas_call(
        functools.partial(_ffn_kernel, emit_x=emit_x),
        grid=(n // tm, fp // tf),
        in_specs=[row, vec,
                  pl.BlockSpec((d, tf), lambda i, j: (0, j)),
                  pl.BlockSpec((d, tf), lambda i, j: (0, j)),
                  pl.BlockSpec((tf, d), lambda i, j: (j, 0)),
                  vec],
        out_specs=out_specs,
        out_shape=out_shape,
        scratch_shapes=[pltpu.VMEM((tm, d), BF16), pltpu.VMEM((tm, d), F32)],
        compiler_params=_cparams(("parallel", "arbitrary")),
        name="ffn_half_step",
    )(x, g.reshape(1, d), wg, wu, wd, post_g.reshape(1, d))


def _inproj_kernel(h_ref, w_ref, za_ref, za16_ref, zh_ref, zg_ref, kvc_ref, kvs_ref, kvw_ref, *, na, nh):
    j = pl.program_id(1)
    r = _dot(h_ref[...], w_ref[...])

    @pl.when(j < na)
    def _():
        za_ref[...] = r
        za16_ref[...] = r.astype(BF16)

    for col, kv_ref in ((COL_CMP, kvc_ref), (COL_SLC, kvs_ref), (COL_WIN, kvw_ref)):
        @pl.when(j == col // PROJ_TN)
        def _(kv_ref=kv_ref):
            for ck in range(KV_GROUPS):
                kv_ref[pl.ds(ck, r.shape[0], stride=KV_GROUPS), :] = r[:, ck * HEAD_DIM:(ck + 1) * HEAD_DIM]

    @pl.when((j >= na) & (j < na + nh))
    def _():
        zh_ref[...] = r

    @pl.when(j == na + nh)
    def _():
        zg_ref[...] = r[:, :ZG_COLS]


def _inproj(h, w):
    n, d = h.shape
    tn = PROJ_TN
    na, nh = ZA_COLS // tn, ZH_COLS // tn
    tm = min(n, 1024)
    assert n % tm == 0 and w.shape[1] == (na + nh + 1) * tn and tn == KV_COLS
    kv_spec = pl.BlockSpec((tm * KV_GROUPS, HEAD_DIM), lambda i, j: (i, 0))
    kv_shape = jax.ShapeDtypeStruct((n * KV_GROUPS, HEAD_DIM), F32)
    return pl.pallas_call(
        functools.partial(_inproj_kernel, na=na, nh=nh),
        grid=(n // tm, na + nh + 1),
        in_specs=[pl.BlockSpec((tm, d), lambda i, j: (i, 0)),
                  pl.BlockSpec((d, tn), lambda i, j: (0, j))],
        out_specs=[pl.BlockSpec((tm, tn), lambda i, j: (i, jnp.minimum(j, na - 1))),
                   pl.BlockSpec((tm, tn), lambda i, j: (i, jnp.minimum(j, na - 1))),
                   pl.BlockSpec((tm, tn), lambda i, j: (i, jnp.clip(j - na, 0, nh - 1))),
                   pl.BlockSpec((tm, ZG_COLS), lambda i, j: (i, 0)),
                   kv_spec, kv_spec, kv_spec],
        out_shape=[jax.ShapeDtypeStruct((n, ZA_COLS), F32),
                   jax.ShapeDtypeStruct((n, ZA_COLS), BF16),
                   jax.ShapeDtypeStruct((n, ZH_COLS), F32),
                   jax.ShapeDtypeStruct((n, ZG_COLS), F32),
                   kv_shape, kv_shape, kv_shape],
        compiler_params=_cparams(("parallel", "arbitrary")),
        name="in_projection",
    )(h, w)


def _outproj_kernel(x_ref, a_ref, b_ref, w_ref, o_ref):
    half = a_ref.shape[1]
    o_ref[...] = (x_ref[...]
                  + _dot(a_ref[...].astype(BF16), w_ref[0:half, :])
                  + _dot(b_ref[...].astype(BF16), w_ref[half:, :]))


def _outproj(x, a, b, w):
    n, d = x.shape
    half = a.shape[1]
    tm = min(n, 512)
    assert n % tm == 0
    return pl.pallas_call(
        _outproj_kernel,
        grid=(n // tm,),
        in_specs=[pl.BlockSpec((tm, d), lambda i: (i, 0)),
                  pl.BlockSpec((tm, half), lambda i: (i, 0)),
                  pl.BlockSpec((tm, half), lambda i: (i, 0)),
                  pl.BlockSpec((2 * half, d), lambda i: (0, 0))],
        out_specs=pl.BlockSpec((tm, d), lambda i: (i, 0)),
        out_shape=jax.ShapeDtypeStruct((n, d), F32),
        compiler_params=_cparams(("parallel",)),
        name="out_projection",
    )(x, a, b, w)


def _cmp_bias_kernel(pe_ref, w1_ref, b1_ref, o_ref):
    for c in range(2):
        pe8 = jnp.broadcast_to(pe_ref[c], (8, pe_ref.shape[2])).astype(BF16)
        o_ref[c] = _dot(pe8, w1_ref[c])[0:1, :] + b1_ref[c]


def _cmp_bias(pe, w1, b1):
    flat = CMP_BLOCK * HEAD_DIM
    return pl.pallas_call(
        _cmp_bias_kernel,
        out_shape=jax.ShapeDtypeStruct((2, 1, HEAD_DIM), F32),
        name="cmp_bias",
    )(pe.reshape(2, 1, flat), w1.reshape(2, flat, HEAD_DIM).astype(BF16), b1.reshape(2, 1, HEAD_DIM))


def _cmp_accumulate(get_rows, w1r_ref, hp_sc, seg0, nss):
    for c in range(2):
        acc = None
        for r in range(CMP_STRIDE):
            xr = jnp.concatenate([get_rows(r, 2 * c + k) for k in range(NSA_KV)], axis=0).astype(BF16)
            d = _dot(xr, w1r_ref[c, r])
            acc = d if acc is None else acc + d
        for k in range(NSA_KV):
            hp_sc[2 * c + k, pl.ds(seg0, nss), :] = acc[k * nss:(k + 1) * nss]


def _cmp_finish(hp_sc, bias_ref, w2_ref, o_ref):
    nseg = hp_sc.shape[1]
    last = lax.broadcasted_iota(jnp.int32, (nseg, 1), 0) == nseg - 1
    for ck in range(2 * NSA_KV):
        c = ck // NSA_KV
        first = hp_sc[ck, :, 0:HEAD_DIM]
        second = pltpu.roll(hp_sc[ck, :, HEAD_DIM:2 * HEAD_DIM], nseg - 1, 0)
        pre = first + second + bias_ref[c]
        out = _dot((pre * jax.nn.sigmoid(pre)).astype(BF16), w2_ref[c])
        o_ref[0, ck] = jnp.where(last, 0.0, out).astype(o_ref.dtype)


def _cmp_prompt_kernel(x0_ref, x1_ref, x2_ref, x3_ref, w1r_ref, bias_ref, w2_ref, o_ref, hp_sc, *, nss):
    j = pl.program_id(1)
    x_refs = (x0_ref, x1_ref, x2_ref, x3_ref)

    def get_rows(r, ck):
        return x_refs[ck][pl.ds(r, nss, stride=CMP_STRIDE), :]

    _cmp_accumulate(get_rows, w1r_ref, hp_sc, pl.multiple_of(j * nss, nss), nss)

    @pl.when(j == pl.num_programs(1) - 1)
    def _():
        _cmp_finish(hp_sc, bias_ref, w2_ref, o_ref)


def _cmp_weight_specs(grid_rank):
    zeros = lambda n: (lambda *a: (0,) * n)
    return [pl.BlockSpec((2, CMP_STRIDE, HEAD_DIM, 2 * HEAD_DIM), zeros(4)),
            pl.BlockSpec((2, 1, HEAD_DIM), zeros(3)),
            pl.BlockSpec((2, HEAD_DIM, HEAD_DIM), zeros(3))]


def _cmp_prompt(za, b, t, w1r, bias, w2):
    rc = min(t, 2048)
    nss = rc // CMP_STRIDE
    nseg = t // CMP_STRIDE
    steps = t // rc
    assert t % rc == 0
    return pl.pallas_call(
        functools.partial(_cmp_prompt_kernel, nss=nss),
        grid=(b, steps),
        in_specs=[pl.BlockSpec((rc, HEAD_DIM), functools.partial(lambda i, j, ck: (i * steps + j, COL_CMP // HEAD_DIM + ck), ck=ck))
                  for ck in range(2 * NSA_KV)] + _cmp_weight_specs(2),
        out_specs=pl.BlockSpec((1, 2 * NSA_KV, nseg, HEAD_DIM), lambda i, j: (i, 0, 0, 0)),
        out_shape=jax.ShapeDtypeStruct((b, 2 * NSA_KV, nseg, HEAD_DIM), BF16),
        scratch_shapes=[pltpu.VMEM((2 * NSA_KV, nseg, 2 * HEAD_DIM), F32)],
        compiler_params=_cparams(("parallel", "arbitrary")),
        name="cmp_mlp_prompt",
    )(za, za, za, za, w1r, bias, w2)


CMP_PAGES = 16


KV_GROUPS = 2 * NSA_KV
PAGE_ROWS = PAGE_SIZE * KV_GROUPS


def _cmp_paged_kernel(pt_ref, *refs):
    pages = refs[:CMP_PAGES]
    perm_ref, w1p_ref, bias_ref, w2_ref, o_ref, hp_sc, x_sc = refs[CMP_PAGES:]
    j = pl.program_id(1)
    spp = PAGE_SIZE // CMP_STRIDE
    nss = CMP_PAGES * spp
    seg0 = pl.multiple_of(j * nss, nss)
    for c in range(2):
        for g, pg in enumerate(pages):
            xc = jnp.concatenate([pg[pl.ds(NSA_KV * c + k, PAGE_SIZE, stride=KV_GROUPS), :] for k in range(NSA_KV)],
                                 axis=1).astype(BF16)
            xp = _dot(perm_ref[...], xc)
            for r in range(CMP_STRIDE):
                for k in range(NSA_KV):
                    x_sc[r, k, g * spp:(g + 1) * spp, :] = xp[r * spp:(r + 1) * spp, k * HEAD_DIM:(k + 1) * HEAD_DIM]
        acc = None
        for rp in range(CMP_STRIDE // 2):
            lhs = jnp.concatenate([x_sc[2 * rp].reshape(NSA_KV * nss, HEAD_DIM),
                                   x_sc[2 * rp + 1].reshape(NSA_KV * nss, HEAD_DIM)], axis=1).astype(BF16)
            d = _dot(lhs, w1p_ref[c, rp])
            acc = d if acc is None else acc + d
        for k in range(NSA_KV):
            hp_sc[2 * c + k, pl.ds(seg0, nss), :] = acc[k * nss:(k + 1) * nss]

    @pl.when(j == pl.num_programs(1) - 1)
    def _():
        _cmp_finish(hp_sc, bias_ref, w2_ref, o_ref)


def _page_specs(n, page0):
    def spec(g):
        return pl.BlockSpec((PAGE_ROWS, HEAD_DIM), lambda i, j, pt: (page0 + pt[i, j * n + g], 0))
    return [spec(g) for g in range(n)]


def _cmp_paged(cache, page0, page_table, w1r, bias, w2):
    db, n_pages = page_table.shape
    assert n_pages % CMP_PAGES == 0 and PAGE_SIZE == HEAD_DIM
    nseg = n_pages * PAGE_SIZE // CMP_STRIDE
    spp = PAGE_SIZE // CMP_STRIDE
    nss = CMP_PAGES * spp
    i = jnp.arange(PAGE_SIZE)
    perm = (i[None, :] == ((i % spp) * CMP_STRIDE + i // spp)[:, None]).astype(BF16)
    w1p = w1r.reshape(2, CMP_STRIDE // 2, 2 * HEAD_DIM, 2 * HEAD_DIM)
    zeros = lambda n: (lambda i, j, pt: (0,) * n)
    grid_spec = pltpu.PrefetchScalarGridSpec(
        num_scalar_prefetch=1,
        grid=(db, n_pages // CMP_PAGES),
        in_specs=_page_specs(CMP_PAGES, page0) + [
            pl.BlockSpec((PAGE_SIZE, PAGE_SIZE), zeros(2)),
            pl.BlockSpec((2, CMP_STRIDE // 2, 2 * HEAD_DIM, 2 * HEAD_DIM), zeros(4)),
            pl.BlockSpec((2, 1, HEAD_DIM), zeros(3)),
            pl.BlockSpec((2, HEAD_DIM, HEAD_DIM), zeros(3))],
        out_specs=pl.BlockSpec((1, 2 * NSA_KV, nseg, HEAD_DIM), lambda i, j, pt: (i, 0, 0, 0)),
        scratch_shapes=[pltpu.VMEM((2 * NSA_KV, nseg, 2 * HEAD_DIM), F32),
                        pltpu.VMEM((CMP_STRIDE, NSA_KV, nss, HEAD_DIM), F32)])
    return pl.pallas_call(
        _cmp_paged_kernel,
        grid_spec=grid_spec,
        out_shape=jax.ShapeDtypeStruct((db, 2 * NSA_KV, nseg, HEAD_DIM), BF16),
        compiler_params=_cparams(("parallel", "arbitrary")),
        name="cmp_mlp_paged",
    )(page_table, *([cache] * CMP_PAGES), perm, w1p, bias, w2)


def _softmax0(s):
    m = jnp.max(s, axis=0, keepdims=True)
    m = jnp.where(m > NEG_INF, m, 0.0)
    e = jnp.exp2((s - m) * EXP2_SCALE)
    l = jnp.sum(e, axis=0, keepdims=True)
    return e * (1.0 / jnp.maximum(l, 1e-30))


def _cmp_attn_t(ck, cv, q, tpos):
    ncp = ck.shape[0]
    s = _dot_nt(ck, q)
    cend = lax.broadcasted_iota(jnp.int32, (ncp, 1), 0) * CMP_STRIDE + (CMP_BLOCK - 1)
    p = _softmax0(jnp.where(cend <= tpos, s, NEG_INF))
    return p, _dot_tn(cv, p.astype(BF16))


def _sel_importance_t(msel, u):
    hi, mid, lo = _split3(u)
    return _dot(msel, hi) + _dot(msel, mid) + _dot(msel, lo)


def _topk_t(pslc, tpos, nsel):
    nsp = pslc.shape[0]
    jio = lax.broadcasted_iota(jnp.int32, pslc.shape, 0)
    jt = lax.shift_right_logical(tpos, SEL_SHIFT)
    forced = (jio == 0) | (jio == jt) | (jio == jt - 1)
    score = jnp.where(forced, jnp.inf, jnp.where(jio <= jt, pslc, NEG_INF))
    sel = jnp.zeros(pslc.shape, F32)
    for _ in range(nsel):
        m = jnp.max(score, axis=0, keepdims=True)
        idx = jnp.min(jnp.where(score == m, jio, nsp), axis=0, keepdims=True)
        hit = jio == idx
        score = jnp.where(hit, NEG_INF, score)
        sel = jnp.where(hit, 1.0, sel)
    return sel


def _nsa_prompt_kernel(q_ref, ks_ref, vs_ref, kw_ref, vw_ref, eb_ref, ck_ref, cv_ref, zg_ref, msel_ref, gain_ref,
                       o_ref, gate_sc, s_sc, *, tq, tk, nsel, chains):
    k = pl.program_id(1)
    i = pl.program_id(2)
    t0 = i * tq
    r = NSA_GROUP * tq
    q = q_ref[...]
    qr = jnp.concatenate([q[:, g * HEAD_DIM:(g + 1) * HEAD_DIM] for g in range(NSA_GROUP)], axis=0)
    tpos1 = t0 + lax.broadcasted_iota(jnp.int32, (1, tq), 1)
    tpos = jnp.concatenate([tpos1] * NSA_GROUP, axis=1)

    p, o_cmp = _cmp_attn_t(ck_ref[0, 0], cv_ref[0, 0], qr, tpos)
    imp = p[:, 0:tq]
    for g in range(1, NSA_GROUP):
        imp = imp + p[:, g * tq:(g + 1) * tq]

    wlen = WINDOW + tq
    ws = pl.multiple_of(jnp.maximum(t0 - WINDOW, 0), tq)
    d = tpos - (ws + lax.broadcasted_iota(jnp.int32, (wlen, 1), 0))
    in_window = lax.bitcast_convert_type(d, jnp.uint32) <= WINDOW
    sw = jnp.where(in_window, _dot_nt(kw_ref[pl.ds(ws, wlen), :], qr), NEG_INF)
    ew = jnp.exp2((sw - jnp.max(sw, axis=0, keepdims=True)) * EXP2_SCALE)
    lw = jnp.sum(ew, axis=0, keepdims=True)
    o_win = _dot_tn(vw_ref[pl.ds(ws, wlen), :], ew.astype(BF16)) * (1.0 / jnp.maximum(lw, 1e-30))

    sel = _topk_t(_sel_importance_t(msel_ref[...], imp), tpos1, nsel)

    selm1 = (sel - 1.0).T.astype(BF16)
    q_aug = jnp.concatenate([qr, jnp.concatenate([selm1] * NSA_GROUP, axis=0)], axis=1)
    ones8 = jnp.ones((8, tk), BF16)
    rc = r // chains

    def k_tile(kt):
        off = pl.multiple_of(kt * tk, tk)
        return jnp.concatenate([ks_ref[pl.ds(off, tk), :], eb_ref[pl.ds(off, tk), :]], axis=1)

    def put_scores(c, k_aug):
        s_sc[c] = _dot_nt(k_aug, q_aug[c * rc:(c + 1) * rc])

    def softmax_tile(c, kt, m, causal):
        s = s_sc[c]
        if causal:
            visible = kt * tk + lax.broadcasted_iota(jnp.int32, (tk, 1), 0) <= tpos[:, c * rc:(c + 1) * rc]
            s = jnp.where(visible, s, NEG_INF)
        m_new = jnp.maximum(m, jnp.max(s, axis=0, keepdims=True))
        s = s_sc[c]
        if causal:
            s = jnp.where(visible, s, NEG_INF)
        return m_new, jnp.exp2((s - m_new) * EXP2_SCALE).astype(BF16)

    def accumulate(m, m_new, l, acc, e, v):
        alpha = jnp.exp2((m - m_new) * EXP2_SCALE)
        return alpha * l + _dot(ones8, e)[0:1, :], alpha * acc + _dot_tn(v, e)

    def step(kt, stats):
        v = vs_ref[pl.ds(pl.multiple_of(kt * tk, tk), tk), :]
        k_next = k_tile(kt + 1)
        new = []
        pending = None
        for c in range(chains):
            m, l, acc = stats[c]
            m_new, e = softmax_tile(c, kt, m, False)
            if pending is not None:
                new.append((pending[1],) + accumulate(*pending))
            put_scores(c, k_next)
            pending = (m, m_new, l, acc, e, v)
        new.append((pending[1],) + accumulate(*pending))
        return tuple(new)

    n_full = t0 // tk
    init = tuple((jnp.full((1, rc), NEG_INF, F32), jnp.zeros((1, rc), F32), jnp.zeros((HEAD_DIM, rc), F32))
                 for _ in range(chains))
    k0 = k_tile(0)
    for c in range(chains):
        put_scores(c, k0)
    stats = lax.fori_loop(0, n_full, step, init)
    v = vs_ref[pl.ds(pl.multiple_of(n_full * tk, tk), tk), :]
    final = []
    for c in range(chains):
        m, l, acc = stats[c]
        m_new, e = softmax_tile(c, n_full, m, True)
        final.append(accumulate(m, m_new, l, acc, e, v))
    o_slc = jnp.concatenate([acc * (1.0 / jnp.maximum(l, 1e-30)) for l, acc in final], axis=1)

    gate_sc[...] = jax.nn.sigmoid(zg_ref[...].T[0:gate_sc.shape[0], :])

    def gate_row(branch):
        return jnp.concatenate(
            [gate_sc[pl.ds(branch * NSA_HEADS + k * NSA_GROUP + g, 1), :] for g in range(NSA_GROUP)], axis=1)

    o = gate_row(0) * o_cmp + gate_row(1) * o_slc + gate_row(2) * o_win
    o = o * lax.rsqrt(jnp.mean(o * o, axis=0, keepdims=True) + EPS)
    for g in range(NSA_GROUP):
        og = o[:, g * tq:(g + 1) * tq] * gain_ref[k * NSA_GROUP + g]
        o_ref[:, g * HEAD_DIM:(g + 1) * HEAD_DIM] = og.T.astype(o_ref.dtype)


NSA_TQ = 256
NSA_CHAINS = 2


def _nsa_prompt(za16, ckv, zg, msel, gain, b, t):
    tq = NSA_TQ
    tk = min(t, 512)
    gain_b = jnp.broadcast_to(gain[:, :, None], (NSA_HEADS, HEAD_DIM, tq))
    nq = t // tq
    ncp = ckv.shape[2]
    nsp = msel.shape[0]
    qw = NSA_GROUP * HEAD_DIM
    assert t % tk == 0 and tk % tq == 0 and t >= WINDOW + tq and nsp == 128
    col = lambda base, kk: (base // HEAD_DIM) + kk
    block_bias = jnp.where(jnp.arange(t)[:, None] // SEL_BLOCK == jnp.arange(nsp)[None, :], MASK_BIAS, 0.0).astype(BF16)

    def seq_spec(base):
        return pl.BlockSpec((t, HEAD_DIM), lambda bi, k, i: (bi, col(base, k)))

    return pl.pallas_call(
        functools.partial(_nsa_prompt_kernel, tq=tq, tk=tk, nsel=min(N_SEL, t // SEL_BLOCK), chains=NSA_CHAINS),
        grid=(b, NSA_KV, nq),
        in_specs=[pl.BlockSpec((tq, qw), lambda bi, k, i: (bi * nq + i, k)),
                  seq_spec(COL_SLC), seq_spec(COL_SLC + NSA_KV * HEAD_DIM),
                  seq_spec(COL_WIN), seq_spec(COL_WIN + NSA_KV * HEAD_DIM),
                  pl.BlockSpec((t, nsp), lambda bi, k, i: (0, 0)),
                  pl.BlockSpec((1, 1, ncp, HEAD_DIM), lambda bi, k, i: (bi, k, 0, 0)),
                  pl.BlockSpec((1, 1, ncp, HEAD_DIM), lambda bi, k, i: (bi, NSA_KV + k, 0, 0)),
                  pl.BlockSpec((tq, ZG_COLS), lambda bi, k, i: (bi * nq + i, 0)),
                  pl.BlockSpec((nsp, ncp), lambda bi, k, i: (0, 0)),
                  pl.BlockSpec((NSA_HEADS, HEAD_DIM, tq), lambda bi, k, i: (0, 0, 0))],
        out_specs=pl.BlockSpec((tq, qw), lambda bi, k, i: (bi * nq + i, k)),
        out_shape=jax.ShapeDtypeStruct((b * t, NSA_WIDTH), BF16),
        scratch_shapes=[pltpu.VMEM((32, tq), F32),
                        pltpu.VMEM((NSA_CHAINS, tk, NSA_GROUP * tq // NSA_CHAINS), F32)],
        compiler_params=_cparams(("parallel", "parallel", "arbitrary")),
        name="nsa_prompt",
    )(za16, za16, za16, za16, za16, block_bias, ckv, ckv, zg, msel, gain_b)


SLC_PAGES = 16


def _nsa_sample_kernel(pt_ref, *refs, ts, past_len):
    pages = refs[:SLC_PAGES]
    (z_ref, zg_ref, ckv_ref, win_ref, msel_ref, e_ref, gain_ref,
     o_ref, sel_sc, m_sc, l_sc, acc_sc, ocmp_sc) = refs[SLC_PAGES:]
    j = pl.program_id(1)
    rows = NSA_GROUP * ts
    lanes = 128
    step_keys = SLC_PAGES * PAGE_SIZE
    step_blocks = step_keys // SEL_BLOCK
    z = z_ref[...]

    def q_rows(k):
        return jnp.concatenate([z[:, (k * NSA_GROUP + g) * HEAD_DIM:(k * NSA_GROUP + g + 1) * HEAD_DIM]
                                for g in range(NSA_GROUP)], axis=0).astype(BF16)

    @pl.when(j == 0)
    def _():
        lane = lax.broadcasted_iota(jnp.int32, (1, lanes), 1)
        tpos = past_len + jnp.bitwise_and(lane, ts - 1)
        s = None
        for k in range(NSA_KV):
            pads = (k * rows, lanes - (k + 1) * rows)
            parts = [jnp.zeros((pads[0], HEAD_DIM), BF16), q_rows(k), jnp.zeros((pads[1], HEAD_DIM), BF16)]
            qp = jnp.concatenate([x for x in parts if x.shape[0] > 0], axis=0)
            sk = _dot_nt(ckv_ref[0, k], qp)
            s = sk if s is None else s + sk
        ncp = s.shape[0]
        cend = lax.broadcasted_iota(jnp.int32, (ncp, 1), 0) * CMP_STRIDE + (CMP_BLOCK - 1)
        p = _softmax0(jnp.where(cend <= tpos, s, NEG_INF))
        p16 = p.astype(BF16)
        for k in range(NSA_KV):
            ocmp_sc[k] = _dot_tn(ckv_ref[0, NSA_KV + k], p16).T[k * rows:(k + 1) * rows, :]
            m_sc[k] = jnp.full((rows, 1), NEG_INF, F32)
            l_sc[k] = jnp.zeros((rows, 1), F32)
            acc_sc[k] = jnp.zeros((rows, HEAD_DIM), F32)
        u = _sel_importance_t(msel_ref[...], p)
        pslc = u
        for g in range(1, NSA_GROUP):
            pslc = pslc + pltpu.roll(u, lanes - g * ts, 1)
        sel_sc[...] = _topk_t(pslc, tpos, N_SEL)

    def online(k, s, v):
        m_old = m_sc[k]
        m_new = jnp.maximum(m_old, jnp.max(s, axis=1, keepdims=True))
        alpha = jnp.exp2((m_old - m_new) * EXP2_SCALE)
        e = jnp.exp2((s - m_new) * EXP2_SCALE)
        l_sc[k] = alpha * l_sc[k] + jnp.sum(e, axis=1, keepdims=True)
        acc_sc[k] = alpha * acc_sc[k] + _dot(e.astype(BF16), v)
        m_sc[k] = m_new

    def kv_rows(ref, ck, n):
        return ref[pl.ds(ck, n, stride=KV_GROUPS), :].astype(BF16)

    selj = sel_sc[pl.ds(pl.multiple_of(j * step_blocks, step_blocks), step_blocks), :]
    mask_all = _dot_tn(selj.astype(BF16), e_ref[...])
    for k in range(NSA_KV):
        kt = jnp.concatenate([kv_rows(pg, k, PAGE_SIZE) for pg in pages], axis=0)
        vt = jnp.concatenate([kv_rows(pg, NSA_KV + k, PAGE_SIZE) for pg in pages], axis=0)
        mask = jnp.concatenate([mask_all[k * rows:k * rows + ts]] * NSA_GROUP, axis=0) > 0.5
        online(k, jnp.where(mask, _dot_nt(q_rows(k), kt), NEG_INF), vt)

    @pl.when(j == pl.num_programs(1) - 1)
    def _():
        row_t = jnp.bitwise_and(lax.broadcasted_iota(jnp.int32, (rows, 1), 0), ts - 1)
        new_i = lax.broadcasted_iota(jnp.int32, (1, 2 * ts), 1)
        new_ok = (new_i < ts) & (new_i <= row_t)
        pad = jnp.zeros((ts, HEAD_DIM), F32)

        def new_rows(col):
            return jnp.concatenate([z[:, col:col + HEAD_DIM], pad], axis=0).astype(BF16)

        wlen = win_ref.shape[0] // KV_GROUPS
        wpos = past_len - wlen + lax.broadcasted_iota(jnp.int32, (1, wlen), 1)
        dw = (past_len + row_t) - wpos
        win_ok = (dw >= 0) & (dw <= WINDOW) & (wpos >= 0)
        zg = zg_ref[...]
        for k in range(NSA_KV):
            q = q_rows(k)
            s_new = _dot_nt(q, new_rows(COL_SLC + k * HEAD_DIM))
            online(k, jnp.where(new_ok, s_new, NEG_INF), new_rows(COL_SLC + (NSA_KV + k) * HEAD_DIM))
            o_slc = acc_sc[k] * (1.0 / jnp.maximum(l_sc[k], 1e-30))
            kw = kv_rows(win_ref, k, wlen)
            vw = kv_rows(win_ref, NSA_KV + k, wlen)
            s1 = jnp.where(win_ok, _dot_nt(q, kw), NEG_INF)
            s2 = jnp.where(new_ok, _dot_nt(q, new_rows(COL_WIN + k * HEAD_DIM)), NEG_INF)
            mw = jnp.maximum(jnp.max(s1, axis=1, keepdims=True), jnp.max(s2, axis=1, keepdims=True))
            e1 = jnp.exp2((s1 - mw) * EXP2_SCALE)
            e2 = jnp.exp2((s2 - mw) * EXP2_SCALE)
            lw = jnp.sum(e1, axis=1, keepdims=True) + jnp.sum(e2, axis=1, keepdims=True)
            o_win = (_dot(e1.astype(BF16), vw)
                     + _dot(e2.astype(BF16), new_rows(COL_WIN + (NSA_KV + k) * HEAD_DIM))) * (1.0 / jnp.maximum(lw, 1e-30))

            def gate(branch):
                c0 = branch * NSA_HEADS + k * NSA_GROUP
                return jnp.concatenate([jax.nn.sigmoid(zg[:, c0 + g:c0 + g + 1]) for g in range(NSA_GROUP)], axis=0)

            o = gate(0) * ocmp_sc[k] + gate(1) * o_slc + gate(2) * o_win
            o = o * lax.rsqrt(jnp.mean(o * o, axis=1, keepdims=True) + EPS)
            for g in range(NSA_GROUP):
                h = k * NSA_GROUP + g
                o_ref[:, h * HEAD_DIM:(h + 1) * HEAD_DIM] = o[g * ts:(g + 1) * ts, :] * gain_ref[h:h + 1, :]


def _nsa_sample(za, zg, ckv, cache_slc, page0, win, page_table, msel, e_blocks, gain, ts):
    db, n_pages = page_table.shape
    past_len = n_pages * PAGE_SIZE
    ncp = ckv.shape[2]
    nsp = msel.shape[0]
    wrows = win.shape[0] // db
    rows = NSA_GROUP * ts
    step_keys = SLC_PAGES * PAGE_SIZE
    assert n_pages % SLC_PAGES == 0 and ts & (ts - 1) == 0 and ts % 8 == 0 and rows <= 128
    assert past_len % SEL_BLOCK == 0 and ts <= SEL_BLOCK and ts < CMP_STRIDE
    zeros = lambda n: (lambda i, j, pt: (0,) * n)
    grid_spec = pltpu.PrefetchScalarGridSpec(
        num_scalar_prefetch=1,
        grid=(db, n_pages // SLC_PAGES),
        in_specs=_page_specs(SLC_PAGES, page0) + [
            pl.BlockSpec((ts, ZA_COLS), lambda i, j, pt: (i, 0)),
            pl.BlockSpec((ts, ZG_COLS), lambda i, j, pt: (i, 0)),
            pl.BlockSpec((1, 2 * NSA_KV, ncp, HEAD_DIM), lambda i, j, pt: (i, 0, 0, 0)),
            pl.BlockSpec((wrows, HEAD_DIM), lambda i, j, pt: (i, 0)),
            pl.BlockSpec((nsp, ncp), zeros(2)),
            pl.BlockSpec((step_keys // SEL_BLOCK, step_keys), zeros(2)),
            pl.BlockSpec((NSA_HEADS, HEAD_DIM), zeros(2))],
        out_specs=pl.BlockSpec((ts, NSA_WIDTH), lambda i, j, pt: (i, 0)),
        scratch_shapes=[pltpu.VMEM((nsp, 128), F32),
                        pltpu.VMEM((NSA_KV, rows, 1), F32),
                        pltpu.VMEM((NSA_KV, rows, 1), F32),
                        pltpu.VMEM((NSA_KV, rows, HEAD_DIM), F32),
                        pltpu.VMEM((NSA_KV, rows, HEAD_DIM), F32)])
    return pl.pallas_call(
        functools.partial(_nsa_sample_kernel, ts=ts, past_len=past_len),
        grid_spec=grid_spec,
        out_shape=jax.ShapeDtypeStruct((db * ts, NSA_WIDTH), F32),
        compiler_params=_cparams(("parallel", "arbitrary")),
        name="nsa_sample",
    )(page_table, *([cache_slc] * SLC_PAGES), za, zg, ckv, win, msel, e_blocks, gain)


def _cumsum0(x):
    n = x.shape[0]
    row = lax.broadcasted_iota(jnp.int32, (n, 1), 0)
    sh = 1
    while sh < n:
        x = x + jnp.where(row >= sh, pltpu.roll(x, sh, 0), 0.0)
        sh *= 2
    return x


HG_PAR = 4


def _hgrn_head_chunk(hq, hf, v, gt, lb, gain, st, *, c, sb):
    fg = lb + (1.0 - lb) * jax.nn.sigmoid(hf)
    kk = 1.0 - fg
    bc = _cumsum0(jnp.log2(fg))
    qh = hq * jax.nn.sigmoid(hq) * HG_SCALE
    o_inter = _dot_nt((qh * jnp.exp2(bc)).astype(BF16), st.astype(BF16))
    v16 = v.astype(BF16)
    row8 = lax.broadcasted_iota(jnp.int32, (8, 1), 0)
    pieces = []
    for si in range(c // sb):
        lo = si * sb
        od = [jnp.zeros((8, HEAD_DIM), F32) for _ in range(sb // 8)]
        for s in range(sb):
            ks, vs_, bs = kk[lo + s:lo + s + 1], v[lo + s:lo + s + 1], bc[lo + s:lo + s + 1]
            for p in range(s // 8, sb // 8):
                r0 = lo + 8 * p
                diff = bc[r0:r0 + 8] - bs
                if p == s // 8:
                    diff = jnp.where(row8 >= s % 8, diff, NEG_INF)
                a = jnp.sum(qh[r0:r0 + 8] * ks * jnp.exp2(diff), axis=1, keepdims=True)
                od[p] = od[p] + a * vs_
        od = od[0] if len(od) == 1 else jnp.concatenate(od, axis=0)
        if si > 0:
            bcr = bc[lo - 1:lo]
            qs = (qh[lo:lo + sb] * jnp.exp2(bc[lo:lo + sb] - bcr)).astype(BF16)
            kp = (kk[0:lo] * jnp.exp2(bcr - bc[0:lo])).astype(BF16)
            od = od + _dot(_dot_nt(qs, kp).astype(BF16), v16[0:lo])
        pieces.append(od)
    o = o_inter + (pieces[0] if len(pieces) == 1 else jnp.concatenate(pieces, axis=0))
    bl = bc[c - 1:c]
    st_new = st * jnp.exp2(bl) + _dot_tn(v16, (kk * jnp.exp2(bl - bc)).astype(BF16))
    return _rms(o, gain) * (gt * jax.nn.sigmoid(gt)), st_new


def _hgrn_kernel(q_ref, f_ref, i_ref, g_ref, lbl_ref, on_ref, s0_ref, o_ref, so_ref, st_sc, *, c, sb, layer):
    j = pl.program_id(2)
    nh = st_sc.shape[0]

    @pl.when(j == 0)
    def _():
        for h in range(nh):
            st_sc[h] = s0_ref[0, h].T

    lbl = lbl_ref[...]
    e = jnp.exp(lbl - jnp.max(lbl, axis=0, keepdims=True))
    lb = jnp.sum(e[0:layer + 1], axis=0, keepdims=True) / jnp.sum(e, axis=0, keepdims=True)
    gain = on_ref[...]

    def chunk(ci, carry):
        rows = pl.ds(pl.multiple_of(ci * c, c), c)
        for h in range(nh):
            cols = slice(h * HEAD_DIM, (h + 1) * HEAD_DIM)
            y, st_new = _hgrn_head_chunk(q_ref[rows, cols], f_ref[rows, cols], i_ref[rows, cols], g_ref[rows, cols],
                                         lb[:, cols], gain[:, cols], st_sc[h], c=c, sb=sb)
            st_sc[h] = st_new
            o_ref[rows, cols] = y.astype(o_ref.dtype)
        return carry

    lax.fori_loop(0, q_ref.shape[0] // c, chunk, 0)

    @pl.when(j == pl.num_programs(2) - 1)
    def _():
        for h in range(nh):
            so_ref[0, h] = st_sc[h].T


def _hgrn(zh, lb_logits, out_norm, s0, b, t, layer, out_dtype):
    c = HG_CHUNK if t % HG_CHUNK == 0 else t
    sb = min(HG_SUB, c)
    tc = min(t, 512)
    nt = t // tc
    hw = HG_PAR * HEAD_DIM
    ng = HG_HEADS // HG_PAR
    assert t % tc == 0 and tc % c == 0 and c % sb == 0 and sb % 8 == 0 and HG_HEADS % HG_PAR == 0

    def piece(p):
        return pl.BlockSpec((tc, hw), lambda bi, h, j: (bi * nt + j, p * ng + h))

    st_spec = pl.BlockSpec((1, HG_PAR, HEAD_DIM, HEAD_DIM), lambda bi, h, j: (bi, h, 0, 0))
    nl = lb_logits.shape[0]
    return pl.pallas_call(
        functools.partial(_hgrn_kernel, c=c, sb=sb, layer=layer),
        grid=(b, ng, nt),
        in_specs=[piece(0), piece(1), piece(2), piece(3),
                  pl.BlockSpec((nl, hw), lambda bi, h, j: (0, h)),
                  pl.BlockSpec((1, hw), lambda bi, h, j: (0, h)),
                  st_spec],
        out_specs=[pl.BlockSpec((tc, hw), lambda bi, h, j: (bi * nt + j, h)), st_spec],
        out_shape=[jax.ShapeDtypeStruct((b * t, HG_WIDTH), out_dtype),
                   jax.ShapeDtypeStruct((b, HG_HEADS, HEAD_DIM, HEAD_DIM), F32)],
        scratch_shapes=[pltpu.VMEM((HG_PAR, HEAD_DIM, HEAD_DIM), F32)],
        compiler_params=_cparams(("parallel", "parallel", "arbitrary")),
        name="hgrn2",
    )(zh, zh, zh, zh, lb_logits, out_norm.reshape(1, HG_WIDTH), s0)


def _sel_coverage(ns_pad, ns, ncp, nc):
    j = jnp.arange(ns_pad)[:, None]
    i = jnp.arange(ncp)[None, :]
    d = i - (SEL_RATIO * j - 1)
    w = jnp.zeros((ns_pad, ncp), F32)
    for dd, ww in enumerate(SEL_SPAN_W):
        w = jnp.where(d == dd, ww, w)
    return jnp.where((j < ns) & (i < nc), w, 0.0).astype(BF16)


def _pack_w_in(w):
    gates = w[:, ZA_COLS:ZA_COLS + 3 * NSA_HEADS]
    return jnp.concatenate([w[:, :ZA_COLS], w[:, ZA_COLS + 3 * NSA_HEADS:],
                            jnp.pad(gates, ((0, 0), (0, PROJ_TN - 3 * NSA_HEADS)))], axis=1).astype(BF16)


def _pack_ffn(wg, wu, wd, tf=512):
    f = wg.shape[1]
    fp = -(-f // tf) * tf
    padc = ((0, 0), (0, fp - f))
    return (jnp.pad(wg, padc).astype(BF16), jnp.pad(wu, padc).astype(BF16),
            jnp.pad(wd, ((0, fp - f), (0, 0))).astype(BF16))


def kernel(x_prompt, x_sample, cache_cmp_kv, cache_slc_kv, state_win_kv, state_hgrn, page_table, ffn1_norm, ffn1_w_gate, ffn1_w_up, ffn1_w_down, mix_norm, w_in, cmp_pe, cmp_w1, cmp_b1, cmp_w2, nsa_out_norm, hg_lb_logits, hg_out_norm, w_out, ffn2_norm, ffn2_w_gate, ffn2_w_up, ffn2_w_down, final_norm):
    depth = w_in.shape[0]
    assert depth == 1, "single-layer trunk"
    l = 0
    b, t, d = x_prompt.shape
    db, ts, _ = x_sample.shape
    n_pool = cache_cmp_kv.shape[1]
    n_pages = page_table.shape[1]
    past_len = n_pages * PAGE_SIZE
    assert state_win_kv.shape[2] == min(WINDOW, past_len) and t % SEL_BLOCK == 0

    f1 = _pack_ffn(ffn1_w_gate[l], ffn1_w_up[l], ffn1_w_down[l])
    f2 = _pack_ffn(ffn2_w_gate[l], ffn2_w_up[l], ffn2_w_down[l])
    w_in_p = _pack_w_in(w_in[l])
    w_out_p = w_out[l].astype(BF16)
    w1 = cmp_w1[l]
    w1r = (w1.reshape(2, CMP_BLOCK // CMP_STRIDE, CMP_STRIDE, HEAD_DIM, HEAD_DIM)
           .transpose(0, 2, 3, 1, 4).reshape(2, CMP_STRIDE, HEAD_DIM, 2 * HEAD_DIM).astype(BF16))
    w2 = cmp_w2[l].astype(BF16)
    cmp_bias = _cmp_bias(cmp_pe[l], w1, cmp_b1[l])
    gain = nsa_out_norm[l]

    def trunk_front(x):
        x1, h = _ffn(x, ffn1_norm[l], *f1, mix_norm[l], emit_x=True, norm_dtype=BF16)
        return (x1,) + tuple(_inproj(h, w_in_p))

    def trunk_back(x1, o_nsa, o_hg):
        x2 = _outproj(x1, o_nsa, o_hg, w_out_p)
        return _ffn(x2, ffn2_norm[l], *f2, final_norm, emit_x=False, norm_dtype=F32)[0]

    def kv_rows(kv, lead):
        return kv.reshape(1, *lead, 2, NSA_KV, HEAD_DIM)

    x1, za, za16, zh, zg, kvc, kvs, kvw = trunk_front(x_prompt.reshape(b * t, d))
    ns = t // SEL_BLOCK
    ncp = t // CMP_STRIDE
    msel = _sel_coverage(128, ns, ncp, ncp - 1)
    ckv = _cmp_prompt(za, b, t, w1r, cmp_bias, w2)
    o_nsa = _nsa_prompt(za16, ckv, zg, msel, gain, b, t)
    s0 = jnp.zeros((b, HG_HEADS, HEAD_DIM, HEAD_DIM), F32)
    o_hg, hg_p = _hgrn(zh, hg_lb_logits, hg_out_norm[l], s0, b, t, l, BF16)
    y_prompt = trunk_back(x1, o_nsa, o_hg).reshape(b, t, d)
    p_cmp = kv_rows(kvc, (b, t))
    p_slc = kv_rows(kvs, (b, t))
    p_win = kv_rows(kvw, (b, t))[:, :, t - min(WINDOW, t):]

    x1s, zas, _, zhs, zgs, kvc_s, kvs_s, kvw_s = trunk_front(x_sample.reshape(db * ts, d))
    ncp_s = past_len // CMP_STRIDE
    ns_s = -(-(past_len + ts) // SEL_BLOCK)
    ns_pad = -(-ns_s // 8) * 8
    msel_s = _sel_coverage(ns_pad, ns_s, ncp_s, ncp_s - 1)
    step_keys = SLC_PAGES * PAGE_SIZE
    e_blocks = (jnp.arange(step_keys)[None, :] // SEL_BLOCK == jnp.arange(step_keys // SEL_BLOCK)[:, None]).astype(BF16)
    ckv_s = _cmp_paged(cache_cmp_kv.reshape(-1, HEAD_DIM), l * n_pool, page_table, w1r, cmp_bias, w2)
    wlen = state_win_kv.shape[2]
    win = state_win_kv[l].reshape(db * wlen * KV_GROUPS, HEAD_DIM)
    o_nsa_s = _nsa_sample(zas, zgs, ckv_s, cache_slc_kv.reshape(-1, HEAD_DIM), l * n_pool, win,
                          page_table, msel_s, e_blocks, gain, ts)
    o_hg_s, hg_s = _hgrn(zhs, hg_lb_logits, hg_out_norm[l], state_hgrn[l], db, ts, l, F32)
    y_sample = trunk_back(x1s, o_nsa_s, o_hg_s).reshape(db, ts, d)
    s_cmp = kv_rows(kvc_s, (db, ts))
    s_slc = kv_rows(kvs_s, (db, ts))
    s_win = jnp.concatenate([state_win_kv[l:l + 1, :, ts:].astype(F32), kv_rows(kvw_s, (db, ts))], axis=2)

    return (y_prompt, y_sample, p_cmp, p_slc, p_win, hg_p[None].astype(state_hgrn.dtype),
            s_cmp, s_slc, s_win, hg_s[None].astype(state_hgrn.dtype))
```

```python
import functools

import jax
import jax.numpy as jnp
from jax import lax
from jax.experimental import pallas as pl
from jax.experimental.pallas import tpu as pltpu

F32 = jnp.float32
BF16 = jnp.bfloat16
NEG_INF = float("-inf")

HEAD_DIM = 128
NSA_KV = 2
NSA_GROUP = 4
NSA_HEADS = NSA_KV * NSA_GROUP
NSA_WIDTH = NSA_HEADS * HEAD_DIM
KV_COLS = 2 * NSA_KV * HEAD_DIM
HG_HEADS = 8
HG_WIDTH = HG_HEADS * HEAD_DIM
CMP_BLOCK = 32
CMP_STRIDE = 16
SEL_BLOCK = 64
SEL_SHIFT = 6
SEL_RATIO = SEL_BLOCK // CMP_STRIDE
N_SEL = 16
SEL_SPAN_W = (1.0, 2.0, 2.0, 2.0, 1.0)
WINDOW = 512
PAGE_SIZE = 128
HG_CHUNK = 64
HG_SUB = 16
ATTN_SCALE = HEAD_DIM ** -0.5
HG_SCALE = HEAD_DIM ** -0.5
EPS = 1e-6
EXP2_SCALE = ATTN_SCALE * 1.4426950408889634
MASK_BIAS = 2.0 ** 40

ZA_COLS = NSA_WIDTH + 3 * KV_COLS
ZH_COLS = 4 * HG_WIDTH
ZG_COLS = 128
PROJ_TN = 512
COL_CMP = NSA_WIDTH
COL_SLC = NSA_WIDTH + KV_COLS
COL_WIN = NSA_WIDTH + 2 * KV_COLS

VMEM_LIMIT = 56 * 1024 * 1024


def _cparams(sem):
    return pltpu.CompilerParams(dimension_semantics=sem, vmem_limit_bytes=VMEM_LIMIT)


def _rms(x, g):
    return x * lax.rsqrt(jnp.mean(x * x, axis=-1, keepdims=True) + EPS) * g


def _dot(a, b):
    return jnp.dot(a, b, preferred_element_type=F32)


def _dot_nt(a, b):
    return lax.dot_general(a, b, (((1,), (1,)), ((), ())), preferred_element_type=F32)


def _dot_tn(a, b):
    return lax.dot_general(a, b, (((0,), (0,)), ((), ())), preferred_element_type=F32)


def _split3(x):
    hi = x.astype(BF16)
    r1 = x - hi.astype(F32)
    mid = r1.astype(BF16)
    lo = (r1 - mid.astype(F32)).astype(BF16)
    return hi, mid, lo


def _ffn_kernel(x_ref, g_ref, wg_ref, wu_ref, wd_ref, pg_ref, *rest, emit_x):
    if emit_x:
        xo_ref, no_ref, h_sc, acc_sc = rest
    else:
        no_ref, h_sc, acc_sc = rest
    j = pl.program_id(1)

    @pl.when(j == 0)
    def _():
        h_sc[...] = _rms(x_ref[...], g_ref[...]).astype(BF16)
        acc_sc[...] = jnp.zeros_like(acc_sc)

    h = h_sc[...]
    a = _dot(h, wg_ref[...])
    u = _dot(h, wu_ref[...])
    act = (a * jax.nn.sigmoid(a) * u).astype(BF16)
    acc_sc[...] += _dot(act, wd_ref[...])

    @pl.when(j == pl.num_programs(1) - 1)
    def _():
        xn = x_ref[...] + 0.5 * acc_sc[...]
        if emit_x:
            xo_ref[...] = xn
        no_ref[...] = _rms(xn, pg_ref[...]).astype(no_ref.dtype)


def _ffn(x, g, wg, wu, wd, post_g, *, emit_x, norm_dtype, tf=512):
    n, d = x.shape
    fp = wg.shape[1]
    tm = min(n, 512)
    assert n % tm == 0 and fp % tf == 0
    row = pl.BlockSpec((tm, d), lambda i, j: (i, 0))
    vec = pl.BlockSpec((1, d), lambda i, j: (0, 0))
    out_shape = [jax.ShapeDtypeStruct((n, d), norm_dtype)]
    out_specs = [row]
    if emit_x:
        out_shape = [jax.ShapeDtypeStruct((n, d), F32)] + out_shape
        out_specs = [row] + out_specs
    return pl.pallas_call(
        functools.partial(_ffn_kernel, emit_x=emit_x),
        grid=(n // tm, fp // tf),
        in_specs=[row, vec,
                  pl.BlockSpec((d, tf), lambda i, j: (0, j)),
                  pl.BlockSpec((d, tf), lambda i, j: (0, j)),
                  pl.BlockSpec((tf, d), lambda i, j: (j, 0)),
                  vec],
        out_specs=out_specs,
        out_shape=out_shape,
        scratch_shapes=[pltpu.VMEM((tm, d), BF16), pltpu.VMEM((tm, d), F32)],
        compiler_params=_cparams(("parallel", "arbitrary")),
        name="ffn_half_step",
    )(x, g.reshape(1, d), wg, wu, wd, post_g.reshape(1, d))


def _inproj_kernel(h_ref, w_ref, za_ref, za16_ref, zh_ref, zg_ref, kvc_ref, kvs_ref, kvw_ref, *, na, nh):
    j = pl.program_id(1)
    tm = h_ref.shape[0]

    def product():
        return _dot(h_ref[...], w_ref[...])

    @pl.when(j < COL_CMP // PROJ_TN)
    def _():
        r = product()
        za_ref[...] = r
        za16_ref[...] = r.astype(BF16)

    for col, kv_ref in ((COL_CMP, kvc_ref), (COL_SLC, kvs_ref), (COL_WIN, kvw_ref)):
        @pl.when(j == col // PROJ_TN)
        def _(kv_ref=kv_ref):
            r = product()
            za_ref[...] = r
            za16_ref[...] = r.astype(BF16)
            for ck in range(KV_GROUPS):
                kv_ref[pl.ds(ck, tm, stride=KV_GROUPS), :] = r[:, ck * HEAD_DIM:(ck + 1) * HEAD_DIM]

    @pl.when((j >= na) & (j < na + nh))
    def _():
        zh_ref[...] = product()

    @pl.when(j == na + nh)
    def _():
        zg_ref[...] = product()[:, :ZG_COLS]


def _inproj(h, w):
    n, d = h.shape
    tn = PROJ_TN
    na, nh = ZA_COLS // tn, ZH_COLS // tn
    tm = min(n, 1024)
    assert n % tm == 0 and w.shape[1] == (na + nh + 1) * tn and tn == KV_COLS
    kv_spec = pl.BlockSpec((tm * KV_GROUPS, HEAD_DIM), lambda i, j: (i, 0))
    kv_shape = jax.ShapeDtypeStruct((n * KV_GROUPS, HEAD_DIM), F32)
    return pl.pallas_call(
        functools.partial(_inproj_kernel, na=na, nh=nh),
        grid=(n // tm, na + nh + 1),
        in_specs=[pl.BlockSpec((tm, d), lambda i, j: (i, 0)),
                  pl.BlockSpec((d, tn), lambda i, j: (0, j))],
        out_specs=[pl.BlockSpec((tm, tn), lambda i, j: (i, jnp.minimum(j, na - 1))),
                   pl.BlockSpec((tm, tn), lambda i, j: (i, jnp.minimum(j, na - 1))),
                   pl.BlockSpec((tm, tn), lambda i, j: (i, jnp.clip(j - na, 0, nh - 1))),
                   pl.BlockSpec((tm, ZG_COLS), lambda i, j: (i, 0)),
                   kv_spec, kv_spec, kv_spec],
        out_shape=[jax.ShapeDtypeStruct((n, ZA_COLS), F32),
                   jax.ShapeDtypeStruct((n, ZA_COLS), BF16),
                   jax.ShapeDtypeStruct((n, ZH_COLS), F32),
                   jax.ShapeDtypeStruct((n, ZG_COLS), F32),
                   kv_shape, kv_shape, kv_shape],
        compiler_params=_cparams(("parallel", "arbitrary")),
        name="in_projection",
    )(h, w)


def _outproj_kernel(x_ref, a_ref, b_ref, w_ref, o_ref):
    half = a_ref.shape[1]
    o_ref[...] = (x_ref[...]
                  + _dot(a_ref[...].astype(BF16), w_ref[0:half, :])
                  + _dot(b_ref[...].astype(BF16), w_ref[half:, :]))


def _outproj(x, a, b, w):
    n, d = x.shape
    half = a.shape[1]
    tm = min(n, 512)
    assert n % tm == 0
    return pl.pallas_call(
        _outproj_kernel,
        grid=(n // tm,),
        in_specs=[pl.BlockSpec((tm, d), lambda i: (i, 0)),
                  pl.BlockSpec((tm, half), lambda i: (i, 0)),
                  pl.BlockSpec((tm, half), lambda i: (i, 0)),
                  pl.BlockSpec((2 * half, d), lambda i: (0, 0))],
        out_specs=pl.BlockSpec((tm, d), lambda i: (i, 0)),
        out_shape=jax.ShapeDtypeStruct((n, d), F32),
        compiler_params=_cparams(("parallel",)),
        name="out_projection",
    )(x, a, b, w)


def _cmp_bias_kernel(pe_ref, w1_ref, b1_ref, o_ref):
    for c in range(2):
        pe8 = jnp.broadcast_to(pe_ref[c], (8, pe_ref.shape[2])).astype(BF16)
        o_ref[c] = _dot(pe8, w1_ref[c])[0:1, :] + b1_ref[c]


def _cmp_bias(pe, w1, b1):
    flat = CMP_BLOCK * HEAD_DIM
    return pl.pallas_call(
        _cmp_bias_kernel,
        out_shape=jax.ShapeDtypeStruct((2, 1, HEAD_DIM), F32),
        name="cmp_bias",
    )(pe.reshape(2, 1, flat), w1.reshape(2, flat, HEAD_DIM).astype(BF16), b1.reshape(2, 1, HEAD_DIM))


def _cmp_accumulate(get_rows, w1r_ref, hp_sc, seg0, nss):
    for c in range(2):
        acc = None
        for r in range(CMP_STRIDE):
            xr = jnp.concatenate([get_rows(r, 2 * c + k) for k in range(NSA_KV)], axis=0).astype(BF16)
            d = _dot(xr, w1r_ref[c, r])
            acc = d if acc is None else acc + d
        for k in range(NSA_KV):
            hp_sc[2 * c + k, pl.ds(seg0, nss), :] = acc[k * nss:(k + 1) * nss]


def _cmp_finish(hp_sc, bias_ref, w2_ref, o_ref):
    nseg = hp_sc.shape[1]
    last = lax.broadcasted_iota(jnp.int32, (nseg, 1), 0) == nseg - 1
    for ck in range(2 * NSA_KV):
        c = ck // NSA_KV
        first = hp_sc[ck, :, 0:HEAD_DIM]
        second = pltpu.roll(hp_sc[ck, :, HEAD_DIM:2 * HEAD_DIM], nseg - 1, 0)
        pre = first + second + bias_ref[c]
        out = _dot((pre * jax.nn.sigmoid(pre)).astype(BF16), w2_ref[c])
        o_ref[0, ck] = jnp.where(last, 0.0, out).astype(o_ref.dtype)


def _cmp_prompt_kernel(x0_ref, x1_ref, x2_ref, x3_ref, w1r_ref, bias_ref, w2_ref, o_ref, hp_sc, *, nss):
    j = pl.program_id(1)
    x_refs = (x0_ref, x1_ref, x2_ref, x3_ref)

    def get_rows(r, ck):
        return x_refs[ck][pl.ds(r, nss, stride=CMP_STRIDE), :]

    _cmp_accumulate(get_rows, w1r_ref, hp_sc, pl.multiple_of(j * nss, nss), nss)

    @pl.when(j == pl.num_programs(1) - 1)
    def _():
        _cmp_finish(hp_sc, bias_ref, w2_ref, o_ref)


def _cmp_weight_specs(grid_rank):
    zeros = lambda n: (lambda *a: (0,) * n)
    return [pl.BlockSpec((2, CMP_STRIDE, HEAD_DIM, 2 * HEAD_DIM), zeros(4)),
            pl.BlockSpec((2, 1, HEAD_DIM), zeros(3)),
            pl.BlockSpec((2, HEAD_DIM, HEAD_DIM), zeros(3))]


def _cmp_prompt(za, b, t, w1r, bias, w2):
    rc = min(t, 2048)
    nss = rc // CMP_STRIDE
    nseg = t // CMP_STRIDE
    steps = t // rc
    assert t % rc == 0
    return pl.pallas_call(
        functools.partial(_cmp_prompt_kernel, nss=nss),
        grid=(b, steps),
        in_specs=[pl.BlockSpec((rc, HEAD_DIM), functools.partial(lambda i, j, ck: (i * steps + j, COL_CMP // HEAD_DIM + ck), ck=ck))
                  for ck in range(2 * NSA_KV)] + _cmp_weight_specs(2),
        out_specs=pl.BlockSpec((1, 2 * NSA_KV, nseg, HEAD_DIM), lambda i, j: (i, 0, 0, 0)),
        out_shape=jax.ShapeDtypeStruct((b, 2 * NSA_KV, nseg, HEAD_DIM), BF16),
        scratch_shapes=[pltpu.VMEM((2 * NSA_KV, nseg, 2 * HEAD_DIM), F32)],
        compiler_params=_cparams(("parallel", "arbitrary")),
        name="cmp_mlp_prompt",
    )(za, za, za, za, w1r, bias, w2)


CMP_PAGES = 16


KV_GROUPS = 2 * NSA_KV
PAGE_ROWS = PAGE_SIZE * KV_GROUPS


def _cmp_paged_kernel(pt_ref, *refs):
    pages = refs[:CMP_PAGES]
    perm_ref, w1p_ref, bias_ref, w2_ref, o_ref, hp_sc, x_sc = refs[CMP_PAGES:]
    j = pl.program_id(1)
    spp = PAGE_SIZE // CMP_STRIDE
    nss = CMP_PAGES * spp
    seg0 = pl.multiple_of(j * nss, nss)
    for c in range(2):
        for g, pg in enumerate(pages):
            xc = jnp.concatenate([pg[pl.ds(NSA_KV * c + k, PAGE_SIZE, stride=KV_GROUPS), :] for k in range(NSA_KV)],
                                 axis=1).astype(BF16)
            xp = _dot(perm_ref[...], xc)
            for r in range(CMP_STRIDE):
                for k in range(NSA_KV):
                    x_sc[c, r, k, g * spp:(g + 1) * spp, :] = xp[r * spp:(r + 1) * spp, k * HEAD_DIM:(k + 1) * HEAD_DIM]
        for k in range(NSA_KV):
            lhs = jnp.concatenate([x_sc[c, r, k] for r in range(CMP_STRIDE)], axis=1).astype(BF16)
            hp_sc[2 * c + k, pl.ds(seg0, nss), :] = _dot(lhs, w1p_ref[c])

    @pl.when(j == pl.num_programs(1) - 1)
    def _():
        _cmp_finish(hp_sc, bias_ref, w2_ref, o_ref)


def _page_specs(n, page0):
    def spec(g):
        return pl.BlockSpec((PAGE_ROWS, HEAD_DIM), lambda i, j, pt: (page0 + pt[i, j * n + g], 0))
    return [spec(g) for g in range(n)]


def _cmp_paged(cache, page0, page_table, w1r, bias, w2):
    db, n_pages = page_table.shape
    assert n_pages % CMP_PAGES == 0 and PAGE_SIZE == HEAD_DIM
    nseg = n_pages * PAGE_SIZE // CMP_STRIDE
    spp = PAGE_SIZE // CMP_STRIDE
    nss = CMP_PAGES * spp
    i = jnp.arange(PAGE_SIZE)
    perm = (i[None, :] == ((i % spp) * CMP_STRIDE + i // spp)[:, None]).astype(BF16)
    w1p = w1r.reshape(2, CMP_STRIDE * HEAD_DIM, 2 * HEAD_DIM)
    zeros = lambda n: (lambda i, j, pt: (0,) * n)
    grid_spec = pltpu.PrefetchScalarGridSpec(
        num_scalar_prefetch=1,
        grid=(db, n_pages // CMP_PAGES),
        in_specs=_page_specs(CMP_PAGES, page0) + [
            pl.BlockSpec((PAGE_SIZE, PAGE_SIZE), zeros(2)),
            pl.BlockSpec((2, CMP_STRIDE * HEAD_DIM, 2 * HEAD_DIM), zeros(3)),
            pl.BlockSpec((2, 1, HEAD_DIM), zeros(3)),
            pl.BlockSpec((2, HEAD_DIM, HEAD_DIM), zeros(3))],
        out_specs=pl.BlockSpec((1, 2 * NSA_KV, nseg, HEAD_DIM), lambda i, j, pt: (i, 0, 0, 0)),
        scratch_shapes=[pltpu.VMEM((2 * NSA_KV, nseg, 2 * HEAD_DIM), F32),
                        pltpu.VMEM((2, CMP_STRIDE, NSA_KV, nss, HEAD_DIM), F32)])
    return pl.pallas_call(
        _cmp_paged_kernel,
        grid_spec=grid_spec,
        out_shape=jax.ShapeDtypeStruct((db, 2 * NSA_KV, nseg, HEAD_DIM), BF16),
        compiler_params=_cparams(("parallel", "arbitrary")),
        name="cmp_mlp_paged",
    )(page_table, *([cache] * CMP_PAGES), perm, w1p, bias, w2)


def _softmax0(s):
    m = jnp.max(s, axis=0, keepdims=True)
    m = jnp.where(m > NEG_INF, m, 0.0)
    e = jnp.exp2((s - m) * EXP2_SCALE)
    l = jnp.sum(e, axis=0, keepdims=True)
    return e * (1.0 / jnp.maximum(l, 1e-30))


def _cmp_attn_t(ck, cv, q, tpos):
    ncp = ck.shape[0]
    s = _dot_nt(ck, q)
    cend = lax.broadcasted_iota(jnp.int32, (ncp, 1), 0) * CMP_STRIDE + (CMP_BLOCK - 1)
    p = _softmax0(jnp.where(cend <= tpos, s, NEG_INF))
    return p, _dot_tn(cv, p.astype(BF16))


def _sel_importance_t(msel, u):
    hi, mid, lo = _split3(u)
    return _dot(msel, hi) + _dot(msel, mid) + _dot(msel, lo)


def _topk_t(pslc, tpos, nsel):
    nsp = pslc.shape[0]
    jio = lax.broadcasted_iota(jnp.int32, pslc.shape, 0)
    jt = lax.shift_right_logical(tpos, SEL_SHIFT)
    forced = (jio == 0) | (jio == jt) | (jio == jt - 1)
    score0 = jnp.where(forced, jnp.inf, jnp.where(jio <= jt, pslc, NEG_INF))
    score = score0
    for _ in range(nsel):
        m = jnp.max(score, axis=0, keepdims=True)
        idx = jnp.min(jnp.where(score == m, jio, nsp), axis=0, keepdims=True)
        score = jnp.where(jio == idx, NEG_INF, score)
    return jnp.where(score != score0, 1.0, 0.0)


def _nsa_prompt_kernel(q_ref, ks_ref, vs_ref, kw_ref, vw_ref, eb_ref, ck_ref, cv_ref, zg_ref, msel_ref, gain_ref,
                       o_ref, gate_sc, s_sc, m_sc, l_sc, acc_sc, *, tq, tk, nsel, chains):
    k = pl.program_id(1)
    i = pl.program_id(2)
    t0 = i * tq
    r = NSA_GROUP * tq
    q = q_ref[...]
    qr = jnp.concatenate([q[:, g * HEAD_DIM:(g + 1) * HEAD_DIM] for g in range(NSA_GROUP)], axis=0)
    tpos1 = t0 + lax.broadcasted_iota(jnp.int32, (1, tq), 1)
    tpos = jnp.concatenate([tpos1] * NSA_GROUP, axis=1)

    p, o_cmp = _cmp_attn_t(ck_ref[0, 0], cv_ref[0, 0], qr, tpos)
    imp = p[:, 0:tq]
    for g in range(1, NSA_GROUP):
        imp = imp + p[:, g * tq:(g + 1) * tq]

    wlen = WINDOW + tq
    ws = pl.multiple_of(jnp.maximum(t0 - WINDOW, 0), tq)
    d = tpos - (ws + lax.broadcasted_iota(jnp.int32, (wlen, 1), 0))
    in_window = lax.bitcast_convert_type(d, jnp.uint32) <= WINDOW
    sw = jnp.where(in_window, _dot_nt(kw_ref[pl.ds(ws, wlen), :], qr), NEG_INF)
    ew = jnp.exp2((sw - jnp.max(sw, axis=0, keepdims=True)) * EXP2_SCALE)
    lw = jnp.sum(ew, axis=0, keepdims=True)
    o_win = _dot_tn(vw_ref[pl.ds(ws, wlen), :], ew.astype(BF16)) * (1.0 / jnp.maximum(lw, 1e-30))

    sel = _topk_t(_sel_importance_t(msel_ref[...], imp), tpos1, nsel)

    selm1 = (sel - 1.0).T.astype(BF16)
    q_aug = jnp.concatenate([qr, jnp.concatenate([selm1] * NSA_GROUP, axis=0)], axis=1)
    ones8 = jnp.ones((8, tk), BF16)
    rc = r // chains

    def k_tile(kt):
        off = pl.multiple_of(kt * tk, tk)
        return jnp.concatenate([ks_ref[pl.ds(off, tk), :], eb_ref[pl.ds(off, tk), :]], axis=1)

    def put_scores(c, k_aug):
        s_sc[c] = _dot_nt(k_aug, q_aug[c * rc:(c + 1) * rc])

    def softmax_tile(c, kt, m, causal):
        s = s_sc[c]
        if causal:
            visible = kt * tk + lax.broadcasted_iota(jnp.int32, (tk, 1), 0) <= tpos[:, c * rc:(c + 1) * rc]
            s = jnp.where(visible, s, NEG_INF)
        m_new = jnp.maximum(m, jnp.max(s, axis=0, keepdims=True))
        s = s_sc[c]
        if causal:
            s = jnp.where(visible, s, NEG_INF)
        return m_new, jnp.exp2((s - m_new) * EXP2_SCALE).astype(BF16)

    def accumulate(c, m, m_new, e, v):
        alpha = jnp.exp2((m - m_new) * EXP2_SCALE)
        l_sc[c] = alpha * l_sc[c] + _dot(ones8, e)[0:1, :]
        acc_sc[c] = alpha * acc_sc[c] + _dot_tn(v, e)
        m_sc[c] = m_new

    def step(kt, carry):
        v = vs_ref[pl.ds(pl.multiple_of(kt * tk, tk), tk), :]
        k_next = k_tile(kt + 1)
        pending = None
        for c in range(chains):
            m = m_sc[c]
            m_new, e = softmax_tile(c, kt, m, False)
            if pending is not None:
                accumulate(*pending)
            put_scores(c, k_next)
            pending = (c, m, m_new, e, v)
        accumulate(*pending)
        return carry

    n_full = t0 // tk
    k0 = k_tile(0)
    for c in range(chains):
        put_scores(c, k0)
        m_sc[c] = jnp.full((1, rc), NEG_INF, F32)
        l_sc[c] = jnp.zeros((1, rc), F32)
        acc_sc[c] = jnp.zeros((HEAD_DIM, rc), F32)
    lax.fori_loop(0, n_full, step, 0)
    v = vs_ref[pl.ds(pl.multiple_of(n_full * tk, tk), tk), :]
    for c in range(chains):
        m = m_sc[c]
        m_new, e = softmax_tile(c, n_full, m, True)
        accumulate(c, m, m_new, e, v)
    o_slc = jnp.concatenate([acc_sc[c] * (1.0 / jnp.maximum(l_sc[c], 1e-30)) for c in range(chains)], axis=1)

    gate_sc[...] = jax.nn.sigmoid(zg_ref[...].T[0:gate_sc.shape[0], :])

    def gate_row(branch):
        return jnp.concatenate(
            [gate_sc[pl.ds(branch * NSA_HEADS + k * NSA_GROUP + g, 1), :] for g in range(NSA_GROUP)], axis=1)

    o = gate_row(0) * o_cmp + gate_row(1) * o_slc + gate_row(2) * o_win
    o = o * lax.rsqrt(jnp.mean(o * o, axis=0, keepdims=True) + EPS)
    for g in range(NSA_GROUP):
        og = o[:, g * tq:(g + 1) * tq] * gain_ref[k * NSA_GROUP + g]
        o_ref[:, g * HEAD_DIM:(g + 1) * HEAD_DIM] = og.T.astype(o_ref.dtype)


NSA_TQ = 256
NSA_CHAINS = 2


def _nsa_prompt(za16, ckv, zg, msel, gain, b, t):
    tq = NSA_TQ
    tk = min(t, 512)
    gain_b = jnp.broadcast_to(gain[:, :, None], (NSA_HEADS, HEAD_DIM, tq))
    nq = t // tq
    ncp = ckv.shape[2]
    nsp = msel.shape[0]
    qw = NSA_GROUP * HEAD_DIM
    assert t % tk == 0 and tk % tq == 0 and t >= WINDOW + tq and nsp == 128
    col = lambda base, kk: (base // HEAD_DIM) + kk
    block_bias = jnp.where(jnp.arange(t)[:, None] // SEL_BLOCK == jnp.arange(nsp)[None, :], MASK_BIAS, 0.0).astype(BF16)

    def seq_spec(base):
        return pl.BlockSpec((t, HEAD_DIM), lambda bi, k, i: (bi, col(base, k)))

    return pl.pallas_call(
        functools.partial(_nsa_prompt_kernel, tq=tq, tk=tk, nsel=min(N_SEL, t // SEL_BLOCK), chains=NSA_CHAINS),
        grid=(b, NSA_KV, nq),
        in_specs=[pl.BlockSpec((tq, qw), lambda bi, k, i: (bi * nq + i, k)),
                  seq_spec(COL_SLC), seq_spec(COL_SLC + NSA_KV * HEAD_DIM),
                  seq_spec(COL_WIN), seq_spec(COL_WIN + NSA_KV * HEAD_DIM),
                  pl.BlockSpec((t, nsp), lambda bi, k, i: (0, 0)),
                  pl.BlockSpec((1, 1, ncp, HEAD_DIM), lambda bi, k, i: (bi, k, 0, 0)),
                  pl.BlockSpec((1, 1, ncp, HEAD_DIM), lambda bi, k, i: (bi, NSA_KV + k, 0, 0)),
                  pl.BlockSpec((tq, ZG_COLS), lambda bi, k, i: (bi * nq + i, 0)),
                  pl.BlockSpec((nsp, ncp), lambda bi, k, i: (0, 0)),
                  pl.BlockSpec((NSA_HEADS, HEAD_DIM, tq), lambda bi, k, i: (0, 0, 0))],
        out_specs=pl.BlockSpec((tq, qw), lambda bi, k, i: (bi * nq + i, k)),
        out_shape=jax.ShapeDtypeStruct((b * t, NSA_WIDTH), BF16),
        scratch_shapes=[pltpu.VMEM((32, tq), F32),
                        pltpu.VMEM((NSA_CHAINS, tk, NSA_GROUP * tq // NSA_CHAINS), F32),
                        pltpu.VMEM((NSA_CHAINS, 1, NSA_GROUP * tq // NSA_CHAINS), F32),
                        pltpu.VMEM((NSA_CHAINS, 1, NSA_GROUP * tq // NSA_CHAINS), F32),
                        pltpu.VMEM((NSA_CHAINS, HEAD_DIM, NSA_GROUP * tq // NSA_CHAINS), F32)],
        compiler_params=_cparams(("parallel", "parallel", "arbitrary")),
        name="nsa_prompt",
    )(za16, za16, za16, za16, za16, block_bias, ckv, ckv, zg, msel, gain_b)


SLC_PAGES = 16


def _nsa_sample_kernel(pt_ref, *refs, ts, past_len):
    pages = refs[:SLC_PAGES]
    (z_ref, zg_ref, ckv_ref, win_ref, msel_ref, e_ref, gain_ref,
     o_ref, sel_sc, m_sc, l_sc, acc_sc, ocmp_sc) = refs[SLC_PAGES:]
    j = pl.program_id(1)
    rows = NSA_GROUP * ts
    lanes = 128
    step_keys = SLC_PAGES * PAGE_SIZE
    step_blocks = step_keys // SEL_BLOCK
    z = z_ref[...]

    def q_rows(k):
        return jnp.concatenate([z[:, (k * NSA_GROUP + g) * HEAD_DIM:(k * NSA_GROUP + g + 1) * HEAD_DIM]
                                for g in range(NSA_GROUP)], axis=0).astype(BF16)

    @pl.when(j == 0)
    def _():
        lane = lax.broadcasted_iota(jnp.int32, (1, lanes), 1)
        tpos = past_len + jnp.bitwise_and(lane, ts - 1)
        s = None
        for k in range(NSA_KV):
            pads = (k * rows, lanes - (k + 1) * rows)
            parts = [jnp.zeros((pads[0], HEAD_DIM), BF16), q_rows(k), jnp.zeros((pads[1], HEAD_DIM), BF16)]
            qp = jnp.concatenate([x for x in parts if x.shape[0] > 0], axis=0)
            sk = _dot_nt(ckv_ref[0, k], qp)
            s = sk if s is None else s + sk
        ncp = s.shape[0]
        cend = lax.broadcasted_iota(jnp.int32, (ncp, 1), 0) * CMP_STRIDE + (CMP_BLOCK - 1)
        p = _softmax0(jnp.where(cend <= tpos, s, NEG_INF))
        p16 = p.astype(BF16)
        for k in range(NSA_KV):
            ocmp_sc[k] = _dot_tn(ckv_ref[0, NSA_KV + k], p16).T[k * rows:(k + 1) * rows, :]
            m_sc[k] = jnp.full((rows, 1), NEG_INF, F32)
            l_sc[k] = jnp.zeros((rows, 1), F32)
            acc_sc[k] = jnp.zeros((rows, HEAD_DIM), F32)
        u = _sel_importance_t(msel_ref[...], p)
        pslc = u
        for g in range(1, NSA_GROUP):
            pslc = pslc + pltpu.roll(u, lanes - g * ts, 1)
        sel_sc[...] = _topk_t(pslc, tpos, N_SEL)

    def online(k, s, v):
        m_old = m_sc[k]
        m_new = jnp.maximum(m_old, jnp.max(s, axis=1, keepdims=True))
        alpha = jnp.exp2((m_old - m_new) * EXP2_SCALE)
        e = jnp.exp2((s - m_new) * EXP2_SCALE)
        l_sc[k] = alpha * l_sc[k] + jnp.sum(e, axis=1, keepdims=True)
        acc_sc[k] = alpha * acc_sc[k] + _dot(e.astype(BF16), v)
        m_sc[k] = m_new

    def kv_rows(ref, ck, n):
        return ref[pl.ds(ck, n, stride=KV_GROUPS), :].astype(BF16)

    selj = sel_sc[pl.ds(pl.multiple_of(j * step_blocks, step_blocks), step_blocks), :]
    mask_all = _dot_tn(selj.astype(BF16), e_ref[...])
    staged = []
    for k in range(NSA_KV):
        kt = jnp.concatenate([kv_rows(pg, k, PAGE_SIZE) for pg in pages], axis=0)
        mask = jnp.concatenate([mask_all[k * rows:k * rows + ts]] * NSA_GROUP, axis=0) > 0.5
        staged.append(jnp.where(mask, _dot_nt(q_rows(k), kt), NEG_INF))
    for k in range(NSA_KV):
        vt = jnp.concatenate([kv_rows(pg, NSA_KV + k, PAGE_SIZE) for pg in pages], axis=0)
        online(k, staged[k], vt)

    @pl.when(j == pl.num_programs(1) - 1)
    def _():
        row_t = jnp.bitwise_and(lax.broadcasted_iota(jnp.int32, (rows, 1), 0), ts - 1)
        new_i = lax.broadcasted_iota(jnp.int32, (1, 2 * ts), 1)
        new_ok = (new_i < ts) & (new_i <= row_t)
        pad = jnp.zeros((ts, HEAD_DIM), F32)

        def new_rows(col):
            return jnp.concatenate([z[:, col:col + HEAD_DIM], pad], axis=0).astype(BF16)

        wlen = win_ref.shape[0] // KV_GROUPS
        wpos = past_len - wlen + lax.broadcasted_iota(jnp.int32, (1, wlen), 1)
        dw = (past_len + row_t) - wpos
        win_ok = (dw >= 0) & (dw <= WINDOW) & (wpos >= 0)
        zg = zg_ref[...]
        for k in range(NSA_KV):
            q = q_rows(k)
            s_new = _dot_nt(q, new_rows(COL_SLC + k * HEAD_DIM))
            online(k, jnp.where(new_ok, s_new, NEG_INF), new_rows(COL_SLC + (NSA_KV + k) * HEAD_DIM))
            o_slc = acc_sc[k] * (1.0 / jnp.maximum(l_sc[k], 1e-30))
            kw = kv_rows(win_ref, k, wlen)
            vw = kv_rows(win_ref, NSA_KV + k, wlen)
            s1 = jnp.where(win_ok, _dot_nt(q, kw), NEG_INF)
            s2 = jnp.where(new_ok, _dot_nt(q, new_rows(COL_WIN + k * HEAD_DIM)), NEG_INF)
            mw = jnp.maximum(jnp.max(s1, axis=1, keepdims=True), jnp.max(s2, axis=1, keepdims=True))
            e1 = jnp.exp2((s1 - mw) * EXP2_SCALE)
            e2 = jnp.exp2((s2 - mw) * EXP2_SCALE)
            lw = jnp.sum(e1, axis=1, keepdims=True) + jnp.sum(e2, axis=1, keepdims=True)
            o_win = (_dot(e1.astype(BF16), vw)
                     + _dot(e2.astype(BF16), new_rows(COL_WIN + (NSA_KV + k) * HEAD_DIM))) * (1.0 / jnp.maximum(lw, 1e-30))

            def gate(branch):
                c0 = branch * NSA_HEADS + k * NSA_GROUP
                return jnp.concatenate([jax.nn.sigmoid(zg[:, c0 + g:c0 + g + 1]) for g in range(NSA_GROUP)], axis=0)

            o = gate(0) * ocmp_sc[k] + gate(1) * o_slc + gate(2) * o_win
            o = o * lax.rsqrt(jnp.mean(o * o, axis=1, keepdims=True) + EPS)
            for g in range(NSA_GROUP):
                h = k * NSA_GROUP + g
                o_ref[:, h * HEAD_DIM:(h + 1) * HEAD_DIM] = o[g * ts:(g + 1) * ts, :] * gain_ref[h:h + 1, :]


def _nsa_sample(za, zg, ckv, cache_slc, page0, win, page_table, msel, e_blocks, gain, ts):
    db, n_pages = page_table.shape
    past_len = n_pages * PAGE_SIZE
    ncp = ckv.shape[2]
    nsp = msel.shape[0]
    wrows = win.shape[0] // db
    rows = NSA_GROUP * ts
    step_keys = SLC_PAGES * PAGE_SIZE
    assert n_pages % SLC_PAGES == 0 and ts & (ts - 1) == 0 and ts % 8 == 0 and rows <= 128
    assert past_len % SEL_BLOCK == 0 and ts <= SEL_BLOCK and ts < CMP_STRIDE
    zeros = lambda n: (lambda i, j, pt: (0,) * n)
    grid_spec = pltpu.PrefetchScalarGridSpec(
        num_scalar_prefetch=1,
        grid=(db, n_pages // SLC_PAGES),
        in_specs=_page_specs(SLC_PAGES, page0) + [
            pl.BlockSpec((ts, ZA_COLS), lambda i, j, pt: (i, 0)),
            pl.BlockSpec((ts, ZG_COLS), lambda i, j, pt: (i, 0)),
            pl.BlockSpec((1, 2 * NSA_KV, ncp, HEAD_DIM), lambda i, j, pt: (i, 0, 0, 0)),
            pl.BlockSpec((wrows, HEAD_DIM), lambda i, j, pt: (i, 0)),
            pl.BlockSpec((nsp, ncp), zeros(2)),
            pl.BlockSpec((step_keys // SEL_BLOCK, step_keys), zeros(2)),
            pl.BlockSpec((NSA_HEADS, HEAD_DIM), zeros(2))],
        out_specs=pl.BlockSpec((ts, NSA_WIDTH), lambda i, j, pt: (i, 0)),
        scratch_shapes=[pltpu.VMEM((nsp, 128), F32),
                        pltpu.VMEM((NSA_KV, rows, 1), F32),
                        pltpu.VMEM((NSA_KV, rows, 1), F32),
                        pltpu.VMEM((NSA_KV, rows, HEAD_DIM), F32),
                        pltpu.VMEM((NSA_KV, rows, HEAD_DIM), F32)])
    return pl.pallas_call(
        functools.partial(_nsa_sample_kernel, ts=ts, past_len=past_len),
        grid_spec=grid_spec,
        out_shape=jax.ShapeDtypeStruct((db * ts, NSA_WIDTH), F32),
        compiler_params=_cparams(("parallel", "arbitrary")),
        name="nsa_sample",
    )(page_table, *([cache_slc] * SLC_PAGES), za, zg, ckv, win, msel, e_blocks, gain)


def _cumsum0(x):
    n = x.shape[0]
    row = lax.broadcasted_iota(jnp.int32, (n, 1), 0)
    sh = 1
    while sh < n:
        x = x + jnp.where(row >= sh, pltpu.roll(x, sh, 0), 0.0)
        sh *= 2
    return x


HG_PAR = 4


def _hgrn_head_chunk(hq, hf, v, gt, lb, gain, st, *, c, sb):
    fg = lb + (1.0 - lb) * jax.nn.sigmoid(hf)
    kk = 1.0 - fg
    bc = _cumsum0(jnp.log2(fg))
    qh = hq * jax.nn.sigmoid(hq) * HG_SCALE
    o_inter = _dot_nt((qh * jnp.exp2(bc)).astype(BF16), st.astype(BF16))
    v16 = v.astype(BF16)
    row8 = lax.broadcasted_iota(jnp.int32, (8, 1), 0)
    pieces = []
    for si in range(c // sb):
        lo = si * sb
        od = [jnp.zeros((8, HEAD_DIM), F32) for _ in range(sb // 8)]
        for s in range(sb):
            ks, vs_, bs = kk[lo + s:lo + s + 1], v[lo + s:lo + s + 1], bc[lo + s:lo + s + 1]
            for p in range(s // 8, sb // 8):
                r0 = lo + 8 * p
                diff = bc[r0:r0 + 8] - bs
                if p == s // 8:
                    diff = jnp.where(row8 >= s % 8, diff, NEG_INF)
                a = jnp.sum(qh[r0:r0 + 8] * ks * jnp.exp2(diff), axis=1, keepdims=True)
                od[p] = od[p] + a * vs_
        od = od[0] if len(od) == 1 else jnp.concatenate(od, axis=0)
        if si > 0:
            bcr = bc[lo - 1:lo]
            qs = (qh[lo:lo + sb] * jnp.exp2(bc[lo:lo + sb] - bcr)).astype(BF16)
            kp = (kk[0:lo] * jnp.exp2(bcr - bc[0:lo])).astype(BF16)
            od = od + _dot(_dot_nt(qs, kp).astype(BF16), v16[0:lo])
        pieces.append(od)
    o = o_inter + (pieces[0] if len(pieces) == 1 else jnp.concatenate(pieces, axis=0))
    bl = bc[c - 1:c]
    st_new = st * jnp.exp2(bl) + _dot_tn(v16, (kk * jnp.exp2(bl - bc)).astype(BF16))
    return _rms(o, gain) * (gt * jax.nn.sigmoid(gt)), st_new


def _hgrn_kernel(q_ref, f_ref, i_ref, g_ref, lbl_ref, on_ref, s0_ref, o_ref, so_ref, st_sc, *, c, sb, layer):
    j = pl.program_id(2)
    nh = st_sc.shape[0]

    @pl.when(j == 0)
    def _():
        for h in range(nh):
            st_sc[h] = s0_ref[0, h].T

    lbl = lbl_ref[...]
    e = jnp.exp(lbl - jnp.max(lbl, axis=0, keepdims=True))
    lb = jnp.sum(e[0:layer + 1], axis=0, keepdims=True) / jnp.sum(e, axis=0, keepdims=True)
    gain = on_ref[...]

    def chunk(ci, carry):
        rows = pl.ds(pl.multiple_of(ci * c, c), c)
        for h in range(nh):
            cols = slice(h * HEAD_DIM, (h + 1) * HEAD_DIM)
            y, st_new = _hgrn_head_chunk(q_ref[rows, cols], f_ref[rows, cols], i_ref[rows, cols], g_ref[rows, cols],
                                         lb[:, cols], gain[:, cols], st_sc[h], c=c, sb=sb)
            st_sc[h] = st_new
            o_ref[rows, cols] = y.astype(o_ref.dtype)
        return carry

    lax.fori_loop(0, q_ref.shape[0] // c, chunk, 0)

    @pl.when(j == pl.num_programs(2) - 1)
    def _():
        for h in range(nh):
            so_ref[0, h] = st_sc[h].T


def _hgrn(zh, lb_logits, out_norm, s0, b, t, layer, out_dtype):
    c = HG_CHUNK if t % HG_CHUNK == 0 else t
    sb = min(HG_SUB, c)
    tc = min(t, 512)
    nt = t // tc
    hw = HG_PAR * HEAD_DIM
    ng = HG_HEADS // HG_PAR
    assert t % tc == 0 and tc % c == 0 and c % sb == 0 and sb % 8 == 0 and HG_HEADS % HG_PAR == 0

    def piece(p):
        return pl.BlockSpec((tc, hw), lambda bi, h, j: (bi * nt + j, p * ng + h))

    st_spec = pl.BlockSpec((1, HG_PAR, HEAD_DIM, HEAD_DIM), lambda bi, h, j: (bi, h, 0, 0))
    nl = lb_logits.shape[0]
    return pl.pallas_call(
        functools.partial(_hgrn_kernel, c=c, sb=sb, layer=layer),
        grid=(b, ng, nt),
        in_specs=[piece(0), piece(1), piece(2), piece(3),
                  pl.BlockSpec((nl, hw), lambda bi, h, j: (0, h)),
                  pl.BlockSpec((1, hw), lambda bi, h, j: (0, h)),
                  st_spec],
        out_specs=[pl.BlockSpec((tc, hw), lambda bi, h, j: (bi * nt + j, h)), st_spec],
        out_shape=[jax.ShapeDtypeStruct((b * t, HG_WIDTH), out_dtype),
                   jax.ShapeDtypeStruct((b, HG_HEADS, HEAD_DIM, HEAD_DIM), F32)],
        scratch_shapes=[pltpu.VMEM((HG_PAR, HEAD_DIM, HEAD_DIM), F32)],
        compiler_params=_cparams(("parallel", "parallel", "arbitrary")),
        name="hgrn2",
    )(zh, zh, zh, zh, lb_logits, out_norm.reshape(1, HG_WIDTH), s0)


def _sel_coverage(ns_pad, ns, ncp, nc):
    j = jnp.arange(ns_pad)[:, None]
    i = jnp.arange(ncp)[None, :]
    d = i - (SEL_RATIO * j - 1)
    w = jnp.zeros((ns_pad, ncp), F32)
    for dd, ww in enumerate(SEL_SPAN_W):
        w = jnp.where(d == dd, ww, w)
    return jnp.where((j < ns) & (i < nc), w, 0.0).astype(BF16)


def _pack_w_in(w):
    gates = w[:, ZA_COLS:ZA_COLS + 3 * NSA_HEADS]
    return jnp.concatenate([w[:, :ZA_COLS], w[:, ZA_COLS + 3 * NSA_HEADS:],
                            jnp.pad(gates, ((0, 0), (0, PROJ_TN - 3 * NSA_HEADS)))], axis=1).astype(BF16)


def _pack_ffn(wg, wu, wd, tf=512):
    f = wg.shape[1]
    fp = -(-f // tf) * tf
    padc = ((0, 0), (0, fp - f))
    return (jnp.pad(wg, padc).astype(BF16), jnp.pad(wu, padc).astype(BF16),
            jnp.pad(wd, ((0, fp - f), (0, 0))).astype(BF16))


def kernel(x_prompt, x_sample, cache_cmp_kv, cache_slc_kv, state_win_kv, state_hgrn, page_table, ffn1_norm, ffn1_w_gate, ffn1_w_up, ffn1_w_down, mix_norm, w_in, cmp_pe, cmp_w1, cmp_b1, cmp_w2, nsa_out_norm, hg_lb_logits, hg_out_norm, w_out, ffn2_norm, ffn2_w_gate, ffn2_w_up, ffn2_w_down, final_norm):
    depth = w_in.shape[0]
    assert depth == 1, "single-layer trunk"
    l = 0
    b, t, d = x_prompt.shape
    db, ts, _ = x_sample.shape
    n_pool = cache_cmp_kv.shape[1]
    n_pages = page_table.shape[1]
    past_len = n_pages * PAGE_SIZE
    assert state_win_kv.shape[2] == min(WINDOW, past_len) and t % SEL_BLOCK == 0

    f1 = _pack_ffn(ffn1_w_gate[l], ffn1_w_up[l], ffn1_w_down[l])
    f2 = _pack_ffn(ffn2_w_gate[l], ffn2_w_up[l], ffn2_w_down[l])
    w_in_p = _pack_w_in(w_in[l])
    w_out_p = w_out[l].astype(BF16)
    w1 = cmp_w1[l]
    w1r = (w1.reshape(2, CMP_BLOCK // CMP_STRIDE, CMP_STRIDE, HEAD_DIM, HEAD_DIM)
           .transpose(0, 2, 3, 1, 4).reshape(2, CMP_STRIDE, HEAD_DIM, 2 * HEAD_DIM).astype(BF16))
    w2 = cmp_w2[l].astype(BF16)
    cmp_bias = _cmp_bias(cmp_pe[l], w1, cmp_b1[l])
    gain = nsa_out_norm[l]

    def trunk_front(x):
        x1, h = _ffn(x, ffn1_norm[l], *f1, mix_norm[l], emit_x=True, norm_dtype=BF16)
        return (x1,) + tuple(_inproj(h, w_in_p))

    def trunk_back(x1, o_nsa, o_hg):
        x2 = _outproj(x1, o_nsa, o_hg, w_out_p)
        return _ffn(x2, ffn2_norm[l], *f2, final_norm, emit_x=False, norm_dtype=F32)[0]

    def kv_rows(kv, lead):
        return kv.reshape(1, *lead, 2, NSA_KV, HEAD_DIM)

    x1, za, za16, zh, zg, kvc, kvs, kvw = trunk_front(x_prompt.reshape(b * t, d))
    ns = t // SEL_BLOCK
    ncp = t // CMP_STRIDE
    msel = _sel_coverage(128, ns, ncp, ncp - 1)
    ckv = _cmp_prompt(za, b, t, w1r, cmp_bias, w2)
    o_nsa = _nsa_prompt(za16, ckv, zg, msel, gain, b, t)
    s0 = jnp.zeros((b, HG_HEADS, HEAD_DIM, HEAD_DIM), F32)
    o_hg, hg_p = _hgrn(zh, hg_lb_logits, hg_out_norm[l], s0, b, t, l, BF16)
    y_prompt = trunk_back(x1, o_nsa, o_hg).reshape(b, t, d)
    p_cmp = kv_rows(kvc, (b, t))
    p_slc = kv_rows(kvs, (b, t))
    p_win = kv_rows(kvw, (b, t))[:, :, t - min(WINDOW, t):]

    x1s, zas, _, zhs, zgs, kvc_s, kvs_s, kvw_s = trunk_front(x_sample.reshape(db * ts, d))
    ncp_s = past_len // CMP_STRIDE
    ns_s = -(-(past_len + ts) // SEL_BLOCK)
    ns_pad = -(-ns_s // 8) * 8
    msel_s = _sel_coverage(ns_pad, ns_s, ncp_s, ncp_s - 1)
    step_keys = SLC_PAGES * PAGE_SIZE
    e_blocks = (jnp.arange(step_keys)[None, :] // SEL_BLOCK == jnp.arange(step_keys // SEL_BLOCK)[:, None]).astype(BF16)
    ckv_s = _cmp_paged(cache_cmp_kv.reshape(-1, HEAD_DIM), l * n_pool, page_table, w1r, cmp_bias, w2)
    wlen = state_win_kv.shape[2]
    win = state_win_kv[l].reshape(db * wlen * KV_GROUPS, HEAD_DIM)
    o_nsa_s = _nsa_sample(zas, zgs, ckv_s, cache_slc_kv.reshape(-1, HEAD_DIM), l * n_pool, win,
                          page_table, msel_s, e_blocks, gain, ts)
    o_hg_s, hg_s = _hgrn(zhs, hg_lb_logits, hg_out_norm[l], state_hgrn[l], db, ts, l, F32)
    y_sample = trunk_back(x1s, o_nsa_s, o_hg_s).reshape(db, ts, d)
    s_cmp = kv_rows(kvc_s, (db, ts))
    s_slc = kv_rows(kvs_s, (db, ts))
    s_win = jnp.concatenate([state_win_kv[l:l + 1, :, ts:].astype(F32), kv_rows(kvw_s, (db, ts))], axis=2)

    return (y_prompt, y_sample, p_cmp, p_slc, p_win, hg_p[None].astype(state_hgrn.dtype),
            s_cmp, s_slc, s_win, hg_s[None].astype(state_hgrn.dtype))
```

```python
import functools

import jax
import jax.numpy as jnp
from jax import lax
from jax.experimental import pallas as pl
from jax.experimental.pallas import tpu as pltpu

F32 = jnp.float32
BF16 = jnp.bfloat16
NEG_INF = float("-inf")

HEAD_DIM = 128
NSA_KV = 2
NSA_GROUP = 4
NSA_HEADS = NSA_KV * NSA_GROUP
NSA_WIDTH = NSA_HEADS * HEAD_DIM
KV_COLS = 2 * NSA_KV * HEAD_DIM
HG_HEADS = 8
HG_WIDTH = HG_HEADS * HEAD_DIM
CMP_BLOCK = 32
CMP_STRIDE = 16
SEL_BLOCK = 64
SEL_SHIFT = 6
SEL_RATIO = SEL_BLOCK // CMP_STRIDE
N_SEL = 16
SEL_SPAN_W = (1.0, 2.0, 2.0, 2.0, 1.0)
WINDOW = 512
PAGE_SIZE = 128
HG_CHUNK = 64
HG_SUB = 16
ATTN_SCALE = HEAD_DIM ** -0.5
HG_SCALE = HEAD_DIM ** -0.5
EPS = 1e-6
EXP2_SCALE = ATTN_SCALE * 1.4426950408889634
MASK_BIAS = 2.0 ** 40

ZA_COLS = NSA_WIDTH + 3 * KV_COLS
ZH_COLS = 4 * HG_WIDTH
ZG_COLS = 128
PROJ_TN = 512
COL_CMP = NSA_WIDTH
COL_SLC = NSA_WIDTH + KV_COLS
COL_WIN = NSA_WIDTH + 2 * KV_COLS

VMEM_LIMIT = 56 * 1024 * 1024


def _cparams(sem):
    return pltpu.CompilerParams(dimension_semantics=sem, vmem_limit_bytes=VMEM_LIMIT)


def _rms(x, g):
    return x * lax.rsqrt(jnp.mean(x * x, axis=-1, keepdims=True) + EPS) * g


def _dot(a, b):
    return jnp.dot(a, b, preferred_element_type=F32)


def _dot_nt(a, b):
    return lax.dot_general(a, b, (((1,), (1,)), ((), ())), preferred_element_type=F32)


def _dot_tn(a, b):
    return lax.dot_general(a, b, (((0,), (0,)), ((), ())), preferred_element_type=F32)


def _split3(x):
    hi = x.astype(BF16)
    r1 = x - hi.astype(F32)
    mid = r1.astype(BF16)
    lo = (r1 - mid.astype(F32)).astype(BF16)
    return hi, mid, lo


def _ffn_kernel(x_ref, g_ref, wg_ref, wu_ref, wd_ref, pg_ref, *rest, emit_x):
    if emit_x:
        xo_ref, no_ref, h_sc, acc_sc = rest
    else:
        no_ref, h_sc, acc_sc = rest
    j = pl.program_id(1)

    @pl.when(j == 0)
    def _():
        h_sc[...] = _rms(x_ref[...], g_ref[...]).astype(BF16)
        acc_sc[...] = jnp.zeros_like(acc_sc)

    h = h_sc[...]
    a = _dot(h, wg_ref[...])
    u = _dot(h, wu_ref[...])
    act = (a * jax.nn.sigmoid(a) * u).astype(BF16)
    acc_sc[...] += _dot(act, wd_ref[...])

    @pl.when(j == pl.num_programs(1) - 1)
    def _():
        xn = x_ref[...] + 0.5 * acc_sc[...]
        if emit_x:
            xo_ref[...] = xn
        no_ref[...] = _rms(xn, pg_ref[...]).astype(no_ref.dtype)


def _ffn(x, g, wg, wu, wd, post_g, *, emit_x, norm_dtype, tf=512):
    n, d = x.shape
    fp = wg.shape[1]
    tm = min(n, 512)
    assert n % tm == 0 and fp % tf == 0
    row = pl.BlockSpec((tm, d), lambda i, j: (i, 0))
    vec = pl.BlockSpec((1, d), lambda i, j: (0, 0))
    out_shape = [jax.ShapeDtypeStruct((n, d), norm_dtype)]
    out_specs = [row]
    if emit_x:
        out_shape = [jax.ShapeDtypeStruct((n, d), F32)] + out_shape
        out_specs = [row] + out_specs
    return pl.pallas_call(
        functools.partial(_ffn_kernel, emit_x=emit_x),
        grid=(n // tm, fp // tf),
        in_specs=[row, vec,
                  pl.BlockSpec((d, tf), lambda i, j: (0, j)),
                  pl.BlockSpec((d, tf), lambda i, j: (0, j)),
                  pl.BlockSpec((tf, d), lambda i, j: (j, 0)),
                  vec],
        out_specs=out_specs,
        out_shape=out_shape,
        scratch_shapes=[pltpu.VMEM((tm, d), BF16), pltpu.VMEM((tm, d), F32)],
        compiler_params=_cparams(("parallel", "arbitrary")),
        name="ffn_half_step",
    )(x, g.reshape(1, d), wg, wu, wd, post_g.reshape(1, d))


def _inproj_kernel(h_ref, w_ref, za_ref, za16_ref, zh_ref, zg_ref, kvc_ref, kvs_ref, kvw_ref, *, na, nh):
    j = pl.program_id(1)
    tm = h_ref.shape[0]

    def product():
        return _dot(h_ref[...], w_ref[...])

    @pl.when(j < COL_CMP // PROJ_TN)
    def _():
        r = product()
        za_ref[...] = r
        za16_ref[...] = r.astype(BF16)

    for col, kv_ref in ((COL_CMP, kvc_ref), (COL_SLC, kvs_ref), (COL_WIN, kvw_ref)):
        @pl.when(j == col // PROJ_TN)
        def _(kv_ref=kv_ref):
            r = product()
            za_ref[...] = r
            za16_ref[...] = r.astype(BF16)
            for ck in range(KV_GROUPS):
                kv_ref[pl.ds(ck, tm, stride=KV_GROUPS), :] = r[:, ck * HEAD_DIM:(ck + 1) * HEAD_DIM]

    @pl.when((j >= na) & (j < na + nh))
    def _():
        zh_ref[...] = product()

    @pl.when(j == na + nh)
    def _():
        zg_ref[...] = product()[:, :ZG_COLS]


def _inproj(h, w):
    n, d = h.shape
    tn = PROJ_TN
    na, nh = ZA_COLS // tn, ZH_COLS // tn
    tm = min(n, 1024)
    assert n % tm == 0 and w.shape[1] == (na + nh + 1) * tn and tn == KV_COLS
    kv_spec = pl.BlockSpec((tm * KV_GROUPS, HEAD_DIM), lambda i, j: (i, 0))
    kv_shape = jax.ShapeDtypeStruct((n * KV_GROUPS, HEAD_DIM), F32)
    return pl.pallas_call(
        functools.partial(_inproj_kernel, na=na, nh=nh),
        grid=(n // tm, na + nh + 1),
        in_specs=[pl.BlockSpec((tm, d), lambda i, j: (i, 0)),
                  pl.BlockSpec((d, tn), lambda i, j: (0, j))],
        out_specs=[pl.BlockSpec((tm, tn), lambda i, j: (i, jnp.minimum(j, na - 1))),
                   pl.BlockSpec((tm, tn), lambda i, j: (i, jnp.minimum(j, na - 1))),
                   pl.BlockSpec((tm, tn), lambda i, j: (i, jnp.clip(j - na, 0, nh - 1))),
                   pl.BlockSpec((tm, ZG_COLS), lambda i, j: (i, 0)),
                   kv_spec, kv_spec, kv_spec],
        out_shape=[jax.ShapeDtypeStruct((n, ZA_COLS), F32),
                   jax.ShapeDtypeStruct((n, ZA_COLS), BF16),
                   jax.ShapeDtypeStruct((n, ZH_COLS), F32),
                   jax.ShapeDtypeStruct((n, ZG_COLS), F32),
                   kv_shape, kv_shape, kv_shape],
        compiler_params=_cparams(("parallel", "arbitrary")),
        name="in_projection",
    )(h, w)


def _outproj_kernel(x_ref, a_ref, b_ref, w_ref, o_ref):
    half = a_ref.shape[1]
    o_ref[...] = (x_ref[...]
                  + _dot(a_ref[...].astype(BF16), w_ref[0:half, :])
                  + _dot(b_ref[...].astype(BF16), w_ref[half:, :]))


def _outproj(x, a, b, w):
    n, d = x.shape
    half = a.shape[1]
    tm = min(n, 512)
    assert n % tm == 0
    return pl.pallas_call(
        _outproj_kernel,
        grid=(n // tm,),
        in_specs=[pl.BlockSpec((tm, d), lambda i: (i, 0)),
                  pl.BlockSpec((tm, half), lambda i: (i, 0)),
                  pl.BlockSpec((tm, half), lambda i: (i, 0)),
                  pl.BlockSpec((2 * half, d), lambda i: (0, 0))],
        out_specs=pl.BlockSpec((tm, d), lambda i: (i, 0)),
        out_shape=jax.ShapeDtypeStruct((n, d), F32),
        compiler_params=_cparams(("parallel",)),
        name="out_projection",
    )(x, a, b, w)


def _cmp_bias_kernel(pe_ref, w1_ref, b1_ref, o_ref):
    for c in range(2):
        pe8 = jnp.broadcast_to(pe_ref[c], (8, pe_ref.shape[2])).astype(BF16)
        o_ref[c] = _dot(pe8, w1_ref[c])[0:1, :] + b1_ref[c]


def _cmp_bias(pe, w1, b1):
    flat = CMP_BLOCK * HEAD_DIM
    return pl.pallas_call(
        _cmp_bias_kernel,
        out_shape=jax.ShapeDtypeStruct((2, 1, HEAD_DIM), F32),
        name="cmp_bias",
    )(pe.reshape(2, 1, flat), w1.reshape(2, flat, HEAD_DIM).astype(BF16), b1.reshape(2, 1, HEAD_DIM))


def _cmp_accumulate(get_rows, w1r_ref, hp_sc, seg0, nss):
    for c in range(2):
        acc = None
        for r in range(CMP_STRIDE):
            xr = jnp.concatenate([get_rows(r, 2 * c + k) for k in range(NSA_KV)], axis=0).astype(BF16)
            d = _dot(xr, w1r_ref[c, r])
            acc = d if acc is None else acc + d
        for k in range(NSA_KV):
            hp_sc[2 * c + k, pl.ds(seg0, nss), :] = acc[k * nss:(k + 1) * nss]


def _cmp_finish(hp_sc, bias_ref, w2_ref, o_ref):
    nseg = hp_sc.shape[1]
    last = lax.broadcasted_iota(jnp.int32, (nseg, 1), 0) == nseg - 1
    for ck in range(2 * NSA_KV):
        c = ck // NSA_KV
        first = hp_sc[ck, :, 0:HEAD_DIM]
        second = pltpu.roll(hp_sc[ck, :, HEAD_DIM:2 * HEAD_DIM], nseg - 1, 0)
        pre = first + second + bias_ref[c]
        out = _dot((pre * jax.nn.sigmoid(pre)).astype(BF16), w2_ref[c])
        o_ref[0, ck] = jnp.where(last, 0.0, out).astype(o_ref.dtype)


def _cmp_prompt_kernel(x0_ref, x1_ref, x2_ref, x3_ref, w1r_ref, bias_ref, w2_ref, o_ref, hp_sc, *, nss):
    j = pl.program_id(1)
    x_refs = (x0_ref, x1_ref, x2_ref, x3_ref)

    def get_rows(r, ck):
        return x_refs[ck][pl.ds(r, nss, stride=CMP_STRIDE), :]

    _cmp_accumulate(get_rows, w1r_ref, hp_sc, pl.multiple_of(j * nss, nss), nss)

    @pl.when(j == pl.num_programs(1) - 1)
    def _():
        _cmp_finish(hp_sc, bias_ref, w2_ref, o_ref)


def _cmp_weight_specs(grid_rank):
    zeros = lambda n: (lambda *a: (0,) * n)
    return [pl.BlockSpec((2, CMP_STRIDE, HEAD_DIM, 2 * HEAD_DIM), zeros(4)),
            pl.BlockSpec((2, 1, HEAD_DIM), zeros(3)),
            pl.BlockSpec((2, HEAD_DIM, HEAD_DIM), zeros(3))]


def _cmp_prompt(za, b, t, w1r, bias, w2):
    rc = min(t, 2048)
    nss = rc // CMP_STRIDE
    nseg = t // CMP_STRIDE
    steps = t // rc
    assert t % rc == 0
    return pl.pallas_call(
        functools.partial(_cmp_prompt_kernel, nss=nss),
        grid=(b, steps),
        in_specs=[pl.BlockSpec((rc, HEAD_DIM), functools.partial(lambda i, j, ck: (i * steps + j, COL_CMP // HEAD_DIM + ck), ck=ck))
                  for ck in range(2 * NSA_KV)] + _cmp_weight_specs(2),
        out_specs=pl.BlockSpec((1, 2 * NSA_KV, nseg, HEAD_DIM), lambda i, j: (i, 0, 0, 0)),
        out_shape=jax.ShapeDtypeStruct((b, 2 * NSA_KV, nseg, HEAD_DIM), BF16),
        scratch_shapes=[pltpu.VMEM((2 * NSA_KV, nseg, 2 * HEAD_DIM), F32)],
        compiler_params=_cparams(("parallel", "arbitrary")),
        name="cmp_mlp_prompt",
    )(za, za, za, za, w1r, bias, w2)


CMP_PAGES = 16


KV_GROUPS = 2 * NSA_KV
PAGE_ROWS = PAGE_SIZE * KV_GROUPS


def _cmp_paged_kernel(pt_ref, *refs):
    pages = refs[:CMP_PAGES]
    perm_ref, w1p_ref, bias_ref, w2_ref, o_ref, hp_sc, x_sc = refs[CMP_PAGES:]
    j = pl.program_id(1)
    spp = PAGE_SIZE // CMP_STRIDE
    nss = CMP_PAGES * spp
    seg0 = pl.multiple_of(j * nss, nss)
    for c in range(2):
        for g, pg in enumerate(pages):
            xc = jnp.concatenate([pg[pl.ds(NSA_KV * c + k, PAGE_SIZE, stride=KV_GROUPS), :] for k in range(NSA_KV)],
                                 axis=1).astype(BF16)
            xp = _dot(perm_ref[...], xc)
            for r in range(CMP_STRIDE):
                for k in range(NSA_KV):
                    x_sc[c, r, k, g * spp:(g + 1) * spp, :] = xp[r * spp:(r + 1) * spp, k * HEAD_DIM:(k + 1) * HEAD_DIM]
    for c in range(2):
        for k in range(NSA_KV):
            lhs = jnp.concatenate([x_sc[c, r, k] for r in range(CMP_STRIDE)], axis=1).astype(BF16)
            hp_sc[2 * c + k, pl.ds(seg0, nss), :] = _dot(lhs, w1p_ref[c])

    @pl.when(j == pl.num_programs(1) - 1)
    def _():
        _cmp_finish(hp_sc, bias_ref, w2_ref, o_ref)


def _page_specs(n, page0):
    def spec(g):
        return pl.BlockSpec((PAGE_ROWS, HEAD_DIM), lambda i, j, pt: (page0 + pt[i, j * n + g], 0))
    return [spec(g) for g in range(n)]


def _cmp_paged(cache, page0, page_table, w1r, bias, w2):
    db, n_pages = page_table.shape
    assert n_pages % CMP_PAGES == 0 and PAGE_SIZE == HEAD_DIM
    nseg = n_pages * PAGE_SIZE // CMP_STRIDE
    spp = PAGE_SIZE // CMP_STRIDE
    nss = CMP_PAGES * spp
    i = jnp.arange(PAGE_SIZE)
    perm = (i[None, :] == ((i % spp) * CMP_STRIDE + i // spp)[:, None]).astype(BF16)
    w1p = w1r.reshape(2, CMP_STRIDE * HEAD_DIM, 2 * HEAD_DIM)
    zeros = lambda n: (lambda i, j, pt: (0,) * n)
    grid_spec = pltpu.PrefetchScalarGridSpec(
        num_scalar_prefetch=1,
        grid=(db, n_pages // CMP_PAGES),
        in_specs=_page_specs(CMP_PAGES, page0) + [
            pl.BlockSpec((PAGE_SIZE, PAGE_SIZE), zeros(2)),
            pl.BlockSpec((2, CMP_STRIDE * HEAD_DIM, 2 * HEAD_DIM), zeros(3)),
            pl.BlockSpec((2, 1, HEAD_DIM), zeros(3)),
            pl.BlockSpec((2, HEAD_DIM, HEAD_DIM), zeros(3))],
        out_specs=pl.BlockSpec((1, 2 * NSA_KV, nseg, HEAD_DIM), lambda i, j, pt: (i, 0, 0, 0)),
        scratch_shapes=[pltpu.VMEM((2 * NSA_KV, nseg, 2 * HEAD_DIM), F32),
                        pltpu.VMEM((2, CMP_STRIDE, NSA_KV, nss, HEAD_DIM), F32)])
    return pl.pallas_call(
        _cmp_paged_kernel,
        grid_spec=grid_spec,
        out_shape=jax.ShapeDtypeStruct((db, 2 * NSA_KV, nseg, HEAD_DIM), BF16),
        compiler_params=_cparams(("parallel", "arbitrary")),
        name="cmp_mlp_paged",
    )(page_table, *([cache] * CMP_PAGES), perm, w1p, bias, w2)


def _softmax0(s):
    m = jnp.max(s, axis=0, keepdims=True)
    m = jnp.where(m > NEG_INF, m, 0.0)
    e = jnp.exp2((s - m) * EXP2_SCALE)
    l = jnp.sum(e, axis=0, keepdims=True)
    return e * (1.0 / jnp.maximum(l, 1e-30))


def _sel_importance_t(msel, u):
    hi, mid, lo = _split3(u)
    return _dot(msel, hi) + _dot(msel, mid) + _dot(msel, lo)


def _topk_t(pslc, tpos, nsel):
    nsp = pslc.shape[0]
    jio = lax.broadcasted_iota(jnp.int32, pslc.shape, 0)
    jt = lax.shift_right_logical(tpos, SEL_SHIFT)
    forced = (jio == 0) | (jio == jt) | (jio == jt - 1)
    score0 = jnp.where(forced, jnp.inf, jnp.where(jio <= jt, pslc, NEG_INF))
    score = score0
    for _ in range(nsel):
        m = jnp.max(score, axis=0, keepdims=True)
        idx = jnp.min(jnp.where(score == m, jio, nsp), axis=0, keepdims=True)
        score = jnp.where(jio == idx, NEG_INF, score)
    return jnp.where(score != score0, 1.0, 0.0)


def _nsa_prompt_kernel(q_ref, ks_ref, vs_ref, kw_ref, vw_ref, eb_ref, ck_ref, cv_ref, zg_ref, msel_ref, gain_ref,
                       o_ref, gate_sc, s_sc, m_sc, l_sc, acc_sc, *, tq, tk, nsel, chains):
    k = pl.program_id(1)
    i = pl.program_id(2)
    t0 = i * tq
    r = NSA_GROUP * tq
    q = q_ref[...]
    qr = jnp.concatenate([q[:, g * HEAD_DIM:(g + 1) * HEAD_DIM] for g in range(NSA_GROUP)], axis=0)
    tpos1 = t0 + lax.broadcasted_iota(jnp.int32, (1, tq), 1)
    tpos = jnp.concatenate([tpos1] * NSA_GROUP, axis=1)

    ncp = ck_ref.shape[2]
    wlen = WINDOW + tq
    ws = pl.multiple_of(jnp.maximum(t0 - WINDOW, 0), tq)
    s_cmp = _dot_nt(ck_ref[0, 0], qr)
    s_win = _dot_nt(kw_ref[pl.ds(ws, wlen), :], qr)

    cend = lax.broadcasted_iota(jnp.int32, (ncp, 1), 0) * CMP_STRIDE + (CMP_BLOCK - 1)
    p = _softmax0(jnp.where(cend <= tpos, s_cmp, NEG_INF))
    imp = p[:, 0:tq]
    for g in range(1, NSA_GROUP):
        imp = imp + p[:, g * tq:(g + 1) * tq]
    pslc = _sel_importance_t(msel_ref[...], imp)
    o_cmp = _dot_tn(cv_ref[0, 0], p.astype(BF16))

    d = tpos - (ws + lax.broadcasted_iota(jnp.int32, (wlen, 1), 0))
    in_window = lax.bitcast_convert_type(d, jnp.uint32) <= WINDOW
    sw = jnp.where(in_window, s_win, NEG_INF)
    ew = jnp.exp2((sw - jnp.max(sw, axis=0, keepdims=True)) * EXP2_SCALE)
    lw = jnp.sum(ew, axis=0, keepdims=True)
    o_win = _dot_tn(vw_ref[pl.ds(ws, wlen), :], ew.astype(BF16)) * (1.0 / jnp.maximum(lw, 1e-30))

    sel = _topk_t(pslc, tpos1, nsel)

    selm1 = (sel - 1.0).T.astype(BF16)
    q_aug = jnp.concatenate([qr, jnp.concatenate([selm1] * NSA_GROUP, axis=0)], axis=1)
    ones8 = jnp.ones((8, tk), BF16)
    rc = r // chains

    def k_tile(kt):
        off = pl.multiple_of(kt * tk, tk)
        return jnp.concatenate([ks_ref[pl.ds(off, tk), :], eb_ref[pl.ds(off, tk), :]], axis=1)

    def put_scores(c, k_aug):
        s_sc[c] = _dot_nt(k_aug, q_aug[c * rc:(c + 1) * rc])

    def softmax_tile(c, kt, m, causal):
        s = s_sc[c]
        if causal:
            visible = kt * tk + lax.broadcasted_iota(jnp.int32, (tk, 1), 0) <= tpos[:, c * rc:(c + 1) * rc]
            s = jnp.where(visible, s, NEG_INF)
        m_new = jnp.maximum(m, jnp.max(s, axis=0, keepdims=True))
        s = s_sc[c]
        if causal:
            s = jnp.where(visible, s, NEG_INF)
        return m_new, jnp.exp2((s - m_new) * EXP2_SCALE).astype(BF16)

    def accumulate(c, m, m_new, e, v):
        alpha = jnp.exp2((m - m_new) * EXP2_SCALE)
        l_sc[c] = alpha * l_sc[c] + _dot(ones8, e)[0:1, :]
        acc_sc[c] = alpha * acc_sc[c] + _dot_tn(v, e)
        m_sc[c] = m_new

    def step(kt, carry):
        v = vs_ref[pl.ds(pl.multiple_of(kt * tk, tk), tk), :]
        k_next = k_tile(kt + 1)
        pending = None
        for c in range(chains):
            m = m_sc[c]
            m_new, e = softmax_tile(c, kt, m, False)
            if pending is not None:
                accumulate(*pending)
            put_scores(c, k_next)
            pending = (c, m, m_new, e, v)
        accumulate(*pending)
        return carry

    n_full = t0 // tk
    k0 = k_tile(0)
    for c in range(chains):
        put_scores(c, k0)
        m_sc[c] = jnp.full((1, rc), NEG_INF, F32)
        l_sc[c] = jnp.zeros((1, rc), F32)
        acc_sc[c] = jnp.zeros((HEAD_DIM, rc), F32)
    lax.fori_loop(0, n_full, step, 0)
    v = vs_ref[pl.ds(pl.multiple_of(n_full * tk, tk), tk), :]
    for c in range(chains):
        m = m_sc[c]
        m_new, e = softmax_tile(c, n_full, m, True)
        accumulate(c, m, m_new, e, v)
    o_slc = jnp.concatenate([acc_sc[c] * (1.0 / jnp.maximum(l_sc[c], 1e-30)) for c in range(chains)], axis=1)

    gate_sc[...] = jax.nn.sigmoid(zg_ref[...].T[0:gate_sc.shape[0], :])

    def gate_row(branch):
        return jnp.concatenate(
            [gate_sc[pl.ds(branch * NSA_HEADS + k * NSA_GROUP + g, 1), :] for g in range(NSA_GROUP)], axis=1)

    o = gate_row(0) * o_cmp + gate_row(1) * o_slc + gate_row(2) * o_win
    o = o * lax.rsqrt(jnp.mean(o * o, axis=0, keepdims=True) + EPS)
    for g in range(NSA_GROUP):
        og = o[:, g * tq:(g + 1) * tq] * gain_ref[k * NSA_GROUP + g]
        o_ref[:, g * HEAD_DIM:(g + 1) * HEAD_DIM] = og.T.astype(o_ref.dtype)


NSA_TQ = 256
NSA_CHAINS = 2


def _nsa_prompt(za16, ckv, zg, msel, gain, b, t):
    tq = NSA_TQ
    tk = min(t, 512)
    gain_b = jnp.broadcast_to(gain[:, :, None], (NSA_HEADS, HEAD_DIM, tq))
    nq = t // tq
    ncp = ckv.shape[2]
    nsp = msel.shape[0]
    qw = NSA_GROUP * HEAD_DIM
    assert t % tk == 0 and tk % tq == 0 and t >= WINDOW + tq and nsp == 128
    col = lambda base, kk: (base // HEAD_DIM) + kk
    block_bias = jnp.where(jnp.arange(t)[:, None] // SEL_BLOCK == jnp.arange(nsp)[None, :], MASK_BIAS, 0.0).astype(BF16)

    def seq_spec(base):
        return pl.BlockSpec((t, HEAD_DIM), lambda bi, k, i: (bi, col(base, k)))

    return pl.pallas_call(
        functools.partial(_nsa_prompt_kernel, tq=tq, tk=tk, nsel=min(N_SEL, t // SEL_BLOCK), chains=NSA_CHAINS),
        grid=(b, NSA_KV, nq),
        in_specs=[pl.BlockSpec((tq, qw), lambda bi, k, i: (bi * nq + i, k)),
                  seq_spec(COL_SLC), seq_spec(COL_SLC + NSA_KV * HEAD_DIM),
                  seq_spec(COL_WIN), seq_spec(COL_WIN + NSA_KV * HEAD_DIM),
                  pl.BlockSpec((t, nsp), lambda bi, k, i: (0, 0)),
                  pl.BlockSpec((1, 1, ncp, HEAD_DIM), lambda bi, k, i: (bi, k, 0, 0)),
                  pl.BlockSpec((1, 1, ncp, HEAD_DIM), lambda bi, k, i: (bi, NSA_KV + k, 0, 0)),
                  pl.BlockSpec((tq, ZG_COLS), lambda bi, k, i: (bi * nq + i, 0)),
                  pl.BlockSpec((nsp, ncp), lambda bi, k, i: (0, 0)),
                  pl.BlockSpec((NSA_HEADS, HEAD_DIM, tq), lambda bi, k, i: (0, 0, 0))],
        out_specs=pl.BlockSpec((tq, qw), lambda bi, k, i: (bi * nq + i, k)),
        out_shape=jax.ShapeDtypeStruct((b * t, NSA_WIDTH), BF16),
        scratch_shapes=[pltpu.VMEM((32, tq), F32),
                        pltpu.VMEM((NSA_CHAINS, tk, NSA_GROUP * tq // NSA_CHAINS), F32),
                        pltpu.VMEM((NSA_CHAINS, 1, NSA_GROUP * tq // NSA_CHAINS), F32),
                        pltpu.VMEM((NSA_CHAINS, 1, NSA_GROUP * tq // NSA_CHAINS), F32),
                        pltpu.VMEM((NSA_CHAINS, HEAD_DIM, NSA_GROUP * tq // NSA_CHAINS), F32)],
        compiler_params=_cparams(("parallel", "parallel", "arbitrary")),
        name="nsa_prompt",
    )(za16, za16, za16, za16, za16, block_bias, ckv, ckv, zg, msel, gain_b)


SLC_PAGES = 16


def _nsa_sample_kernel(pt_ref, *refs, ts, past_len):
    pages = refs[:SLC_PAGES]
    (z_ref, zg_ref, ckv_ref, win_ref, msel_ref, e_ref, gain_ref,
     o_ref, sel_sc, m_sc, l_sc, acc_sc, ocmp_sc) = refs[SLC_PAGES:]
    j = pl.program_id(1)
    rows = NSA_GROUP * ts
    lanes = 128
    step_keys = SLC_PAGES * PAGE_SIZE
    step_blocks = step_keys // SEL_BLOCK
    z = z_ref[...]

    def q_rows(k):
        return jnp.concatenate([z[:, (k * NSA_GROUP + g) * HEAD_DIM:(k * NSA_GROUP + g + 1) * HEAD_DIM]
                                for g in range(NSA_GROUP)], axis=0).astype(BF16)

    @pl.when(j == 0)
    def _():
        lane = lax.broadcasted_iota(jnp.int32, (1, lanes), 1)
        tpos = past_len + jnp.bitwise_and(lane, ts - 1)
        s = None
        for k in range(NSA_KV):
            pads = (k * rows, lanes - (k + 1) * rows)
            parts = [jnp.zeros((pads[0], HEAD_DIM), BF16), q_rows(k), jnp.zeros((pads[1], HEAD_DIM), BF16)]
            qp = jnp.concatenate([x for x in parts if x.shape[0] > 0], axis=0)
            sk = _dot_nt(ckv_ref[0, k], qp)
            s = sk if s is None else s + sk
        ncp = s.shape[0]
        cend = lax.broadcasted_iota(jnp.int32, (ncp, 1), 0) * CMP_STRIDE + (CMP_BLOCK - 1)
        p = _softmax0(jnp.where(cend <= tpos, s, NEG_INF))
        p16 = p.astype(BF16)
        for k in range(NSA_KV):
            ocmp_sc[k] = _dot_tn(ckv_ref[0, NSA_KV + k], p16).T[k * rows:(k + 1) * rows, :]
            m_sc[k] = jnp.full((rows, 1), NEG_INF, F32)
            l_sc[k] = jnp.zeros((rows, 1), F32)
            acc_sc[k] = jnp.zeros((rows, HEAD_DIM), F32)
        u = _sel_importance_t(msel_ref[...], p)
        pslc = u
        for g in range(1, NSA_GROUP):
            pslc = pslc + pltpu.roll(u, lanes - g * ts, 1)
        sel_sc[...] = _topk_t(pslc, tpos, N_SEL)

    def online(k, s, v):
        m_old = m_sc[k]
        m_new = jnp.maximum(m_old, jnp.max(s, axis=1, keepdims=True))
        alpha = jnp.exp2((m_old - m_new) * EXP2_SCALE)
        e = jnp.exp2((s - m_new) * EXP2_SCALE)
        l_sc[k] = alpha * l_sc[k] + jnp.sum(e, axis=1, keepdims=True)
        acc_sc[k] = alpha * acc_sc[k] + _dot(e.astype(BF16), v)
        m_sc[k] = m_new

    def kv_rows(ref, ck, n):
        return ref[pl.ds(ck, n, stride=KV_GROUPS), :].astype(BF16)

    selj = sel_sc[pl.ds(pl.multiple_of(j * step_blocks, step_blocks), step_blocks), :]
    mask_all = _dot_tn(selj.astype(BF16), e_ref[...])
    staged = []
    for k in range(NSA_KV):
        kt = jnp.concatenate([kv_rows(pg, k, PAGE_SIZE) for pg in pages], axis=0)
        mask = jnp.concatenate([mask_all[k * rows:k * rows + ts]] * NSA_GROUP, axis=0) > 0.5
        staged.append(jnp.where(mask, _dot_nt(q_rows(k), kt), NEG_INF))
    for k in range(NSA_KV):
        vt = jnp.concatenate([kv_rows(pg, NSA_KV + k, PAGE_SIZE) for pg in pages], axis=0)
        online(k, staged[k], vt)

    @pl.when(j == pl.num_programs(1) - 1)
    def _():
        row_t = jnp.bitwise_and(lax.broadcasted_iota(jnp.int32, (rows, 1), 0), ts - 1)
        new_i = lax.broadcasted_iota(jnp.int32, (1, 2 * ts), 1)
        new_ok = (new_i < ts) & (new_i <= row_t)
        pad = jnp.zeros((ts, HEAD_DIM), F32)

        def new_rows(col):
            return jnp.concatenate([z[:, col:col + HEAD_DIM], pad], axis=0).astype(BF16)

        wlen = win_ref.shape[0] // KV_GROUPS
        wpos = past_len - wlen + lax.broadcasted_iota(jnp.int32, (1, wlen), 1)
        dw = (past_len + row_t) - wpos
        win_ok = (dw >= 0) & (dw <= WINDOW) & (wpos >= 0)
        zg = zg_ref[...]
        for k in range(NSA_KV):
            q = q_rows(k)
            s_new = _dot_nt(q, new_rows(COL_SLC + k * HEAD_DIM))
            online(k, jnp.where(new_ok, s_new, NEG_INF), new_rows(COL_SLC + (NSA_KV + k) * HEAD_DIM))
            o_slc = acc_sc[k] * (1.0 / jnp.maximum(l_sc[k], 1e-30))
            kw = kv_rows(win_ref, k, wlen)
            vw = kv_rows(win_ref, NSA_KV + k, wlen)
            s1 = jnp.where(win_ok, _dot_nt(q, kw), NEG_INF)
            s2 = jnp.where(new_ok, _dot_nt(q, new_rows(COL_WIN + k * HEAD_DIM)), NEG_INF)
            mw = jnp.maximum(jnp.max(s1, axis=1, keepdims=True), jnp.max(s2, axis=1, keepdims=True))
            e1 = jnp.exp2((s1 - mw) * EXP2_SCALE)
            e2 = jnp.exp2((s2 - mw) * EXP2_SCALE)
            lw = jnp.sum(e1, axis=1, keepdims=True) + jnp.sum(e2, axis=1, keepdims=True)
            o_win = (_dot(e1.astype(BF16), vw)
                     + _dot(e2.astype(BF16), new_rows(COL_WIN + (NSA_KV + k) * HEAD_DIM))) * (1.0 / jnp.maximum(lw, 1e-30))

            def gate(branch):
                c0 = branch * NSA_HEADS + k * NSA_GROUP
                return jnp.concatenate([jax.nn.sigmoid(zg[:, c0 + g:c0 + g + 1]) for g in range(NSA_GROUP)], axis=0)

            o = gate(0) * ocmp_sc[k] + gate(1) * o_slc + gate(2) * o_win
            o = o * lax.rsqrt(jnp.mean(o * o, axis=1, keepdims=True) + EPS)
            for g in range(NSA_GROUP):
                h = k * NSA_GROUP + g
                o_ref[:, h * HEAD_DIM:(h + 1) * HEAD_DIM] = o[g * ts:(g + 1) * ts, :] * gain_ref[h:h + 1, :]


def _nsa_sample(za, zg, ckv, cache_slc, page0, win, page_table, msel, e_blocks, gain, ts):
    db, n_pages = page_table.shape
    past_len = n_pages * PAGE_SIZE
    ncp = ckv.shape[2]
    nsp = msel.shape[0]
    wrows = win.shape[0] // db
    rows = NSA_GROUP * ts
    step_keys = SLC_PAGES * PAGE_SIZE
    assert n_pages % SLC_PAGES == 0 and ts & (ts - 1) == 0 and ts % 8 == 0 and rows <= 128
    assert past_len % SEL_BLOCK == 0 and ts <= SEL_BLOCK and ts < CMP_STRIDE
    zeros = lambda n: (lambda i, j, pt: (0,) * n)
    grid_spec = pltpu.PrefetchScalarGridSpec(
        num_scalar_prefetch=1,
        grid=(db, n_pages // SLC_PAGES),
        in_specs=_page_specs(SLC_PAGES, page0) + [
            pl.BlockSpec((ts, ZA_COLS), lambda i, j, pt: (i, 0)),
            pl.BlockSpec((ts, ZG_COLS), lambda i, j, pt: (i, 0)),
            pl.BlockSpec((1, 2 * NSA_KV, ncp, HEAD_DIM), lambda i, j, pt: (i, 0, 0, 0)),
            pl.BlockSpec((wrows, HEAD_DIM), lambda i, j, pt: (i, 0)),
            pl.BlockSpec((nsp, ncp), zeros(2)),
            pl.BlockSpec((step_keys // SEL_BLOCK, step_keys), zeros(2)),
            pl.BlockSpec((NSA_HEADS, HEAD_DIM), zeros(2))],
        out_specs=pl.BlockSpec((ts, NSA_WIDTH), lambda i, j, pt: (i, 0)),
        scratch_shapes=[pltpu.VMEM((nsp, 128), F32),
                        pltpu.VMEM((NSA_KV, rows, 1), F32),
                        pltpu.VMEM((NSA_KV, rows, 1), F32),
                        pltpu.VMEM((NSA_KV, rows, HEAD_DIM), F32),
                        pltpu.VMEM((NSA_KV, rows, HEAD_DIM), F32)])
    return pl.pallas_call(
        functools.partial(_nsa_sample_kernel, ts=ts, past_len=past_len),
        grid_spec=grid_spec,
        out_shape=jax.ShapeDtypeStruct((db * ts, NSA_WIDTH), F32),
        compiler_params=_cparams(("parallel", "arbitrary")),
        name="nsa_sample",
    )(page_table, *([cache_slc] * SLC_PAGES), za, zg, ckv, win, msel, e_blocks, gain)


def _cumsum0(x):
    n = x.shape[0]
    row = lax.broadcasted_iota(jnp.int32, (n, 1), 0)
    sh = 1
    while sh < n:
        x = x + jnp.where(row >= sh, pltpu.roll(x, sh, 0), 0.0)
        sh *= 2
    return x


HG_PAR = 4


def _hgrn_heads_chunk(heads, *, c, sb):
    row8 = lax.broadcasted_iota(jnp.int32, (8, 1), 0)
    pre = []
    for hq, hf, v, gt, lb, gain, st in heads:
        fg = lb + (1.0 - lb) * jax.nn.sigmoid(hf)
        kk = 1.0 - fg
        bc = _cumsum0(jnp.log2(fg))
        qh = hq * jax.nn.sigmoid(hq) * HG_SCALE
        pre.append((kk, bc, qh, v.astype(BF16)))
    o_inter = [_dot_nt((qh * jnp.exp2(bc)).astype(BF16), head[6].astype(BF16))
               for (kk, bc, qh, v16), head in zip(pre, heads)]
    pieces = [[] for _ in heads]
    for si in range(c // sb):
        lo = si * sb
        scores = []
        if si > 0:
            for kk, bc, qh, v16 in pre:
                bcr = bc[lo - 1:lo]
                qs = (qh[lo:lo + sb] * jnp.exp2(bc[lo:lo + sb] - bcr)).astype(BF16)
                kp = (kk[0:lo] * jnp.exp2(bcr - bc[0:lo])).astype(BF16)
                scores.append(_dot_nt(qs, kp).astype(BF16))
        for hi, ((kk, bc, qh, v16), head) in enumerate(zip(pre, heads)):
            v = head[2]
            od = [jnp.zeros((8, HEAD_DIM), F32) for _ in range(sb // 8)]
            for s in range(sb):
                ks, vs_, bs = kk[lo + s:lo + s + 1], v[lo + s:lo + s + 1], bc[lo + s:lo + s + 1]
                for p in range(s // 8, sb // 8):
                    r0 = lo + 8 * p
                    diff = bc[r0:r0 + 8] - bs
                    if p == s // 8:
                        diff = jnp.where(row8 >= s % 8, diff, NEG_INF)
                    a = jnp.sum(qh[r0:r0 + 8] * ks * jnp.exp2(diff), axis=1, keepdims=True)
                    od[p] = od[p] + a * vs_
            od = od[0] if len(od) == 1 else jnp.concatenate(od, axis=0)
            if si > 0:
                od = od + _dot(scores[hi], v16[0:lo])
            pieces[hi].append(od)
    out = []
    for (kk, bc, qh, v16), head, oi, pc in zip(pre, heads, o_inter, pieces):
        gt, gain, st = head[3], head[5], head[6]
        o = oi + (pc[0] if len(pc) == 1 else jnp.concatenate(pc, axis=0))
        bl = bc[c - 1:c]
        st_new = st * jnp.exp2(bl) + _dot_tn(v16, (kk * jnp.exp2(bl - bc)).astype(BF16))
        out.append((_rms(o, gain) * (gt * jax.nn.sigmoid(gt)), st_new))
    return out


def _hgrn_kernel(q_ref, f_ref, i_ref, g_ref, lbl_ref, on_ref, s0_ref, o_ref, so_ref, st_sc, *, c, sb, layer):
    j = pl.program_id(2)
    nh = st_sc.shape[0]

    @pl.when(j == 0)
    def _():
        for h in range(nh):
            st_sc[h] = s0_ref[0, h].T

    lbl = lbl_ref[...]
    e = jnp.exp(lbl - jnp.max(lbl, axis=0, keepdims=True))
    lb = jnp.sum(e[0:layer + 1], axis=0, keepdims=True) / jnp.sum(e, axis=0, keepdims=True)
    gain = on_ref[...]

    def chunk(ci, carry):
        rows = pl.ds(pl.multiple_of(ci * c, c), c)
        col = lambda h: slice(h * HEAD_DIM, (h + 1) * HEAD_DIM)
        heads = [(q_ref[rows, col(h)], f_ref[rows, col(h)], i_ref[rows, col(h)], g_ref[rows, col(h)],
                  lb[:, col(h)], gain[:, col(h)], st_sc[h]) for h in range(nh)]
        for h, (y, st_new) in enumerate(_hgrn_heads_chunk(heads, c=c, sb=sb)):
            st_sc[h] = st_new
            o_ref[rows, col(h)] = y.astype(o_ref.dtype)
        return carry

    lax.fori_loop(0, q_ref.shape[0] // c, chunk, 0)

    @pl.when(j == pl.num_programs(2) - 1)
    def _():
        for h in range(nh):
            so_ref[0, h] = st_sc[h].T


def _hgrn(zh, lb_logits, out_norm, s0, b, t, layer, out_dtype):
    c = HG_CHUNK if t % HG_CHUNK == 0 else t
    sb = min(HG_SUB, c)
    tc = min(t, 512)
    nt = t // tc
    hw = HG_PAR * HEAD_DIM
    ng = HG_HEADS // HG_PAR
    assert t % tc == 0 and tc % c == 0 and c % sb == 0 and sb % 8 == 0 and HG_HEADS % HG_PAR == 0

    def piece(p):
        return pl.BlockSpec((tc, hw), lambda bi, h, j: (bi * nt + j, p * ng + h))

    st_spec = pl.BlockSpec((1, HG_PAR, HEAD_DIM, HEAD_DIM), lambda bi, h, j: (bi, h, 0, 0))
    nl = lb_logits.shape[0]
    return pl.pallas_call(
        functools.partial(_hgrn_kernel, c=c, sb=sb, layer=layer),
        grid=(b, ng, nt),
        in_specs=[piece(0), piece(1), piece(2), piece(3),
                  pl.BlockSpec((nl, hw), lambda bi, h, j: (0, h)),
                  pl.BlockSpec((1, hw), lambda bi, h, j: (0, h)),
                  st_spec],
        out_specs=[pl.BlockSpec((tc, hw), lambda bi, h, j: (bi * nt + j, h)), st_spec],
        out_shape=[jax.ShapeDtypeStruct((b * t, HG_WIDTH), out_dtype),
                   jax.ShapeDtypeStruct((b, HG_HEADS, HEAD_DIM, HEAD_DIM), F32)],
        scratch_shapes=[pltpu.VMEM((HG_PAR, HEAD_DIM, HEAD_DIM), F32)],
        compiler_params=_cparams(("parallel", "parallel", "arbitrary")),
        name="hgrn2",
    )(zh, zh, zh, zh, lb_logits, out_norm.reshape(1, HG_WIDTH), s0)


def _sel_coverage(ns_pad, ns, ncp, nc):
    j = jnp.arange(ns_pad)[:, None]
    i = jnp.arange(ncp)[None, :]
    d = i - (SEL_RATIO * j - 1)
    w = jnp.zeros((ns_pad, ncp), F32)
    for dd, ww in enumerate(SEL_SPAN_W):
        w = jnp.where(d == dd, ww, w)
    return jnp.where((j < ns) & (i < nc), w, 0.0).astype(BF16)


def _pack_w_in(w):
    gates = w[:, ZA_COLS:ZA_COLS + 3 * NSA_HEADS]
    return jnp.concatenate([w[:, :ZA_COLS], w[:, ZA_COLS + 3 * NSA_HEADS:],
                            jnp.pad(gates, ((0, 0), (0, PROJ_TN - 3 * NSA_HEADS)))], axis=1).astype(BF16)


def _pack_ffn(wg, wu, wd, tf=512):
    f = wg.shape[1]
    fp = -(-f // tf) * tf
    padc = ((0, 0), (0, fp - f))
    return (jnp.pad(wg.astype(BF16), padc), jnp.pad(wu.astype(BF16), padc),
            jnp.pad(wd.astype(BF16), ((0, fp - f), (0, 0))))


def kernel(x_prompt, x_sample, cache_cmp_kv, cache_slc_kv, state_win_kv, state_hgrn, page_table, ffn1_norm, ffn1_w_gate, ffn1_w_up, ffn1_w_down, mix_norm, w_in, cmp_pe, cmp_w1, cmp_b1, cmp_w2, nsa_out_norm, hg_lb_logits, hg_out_norm, w_out, ffn2_norm, ffn2_w_gate, ffn2_w_up, ffn2_w_down, final_norm):
    depth = w_in.shape[0]
    assert depth == 1, "single-layer trunk"
    l = 0
    b, t, d = x_prompt.shape
    db, ts, _ = x_sample.shape
    n_pool = cache_cmp_kv.shape[1]
    n_pages = page_table.shape[1]
    past_len = n_pages * PAGE_SIZE
    assert state_win_kv.shape[2] == min(WINDOW, past_len) and t % SEL_BLOCK == 0

    f1 = _pack_ffn(ffn1_w_gate[l], ffn1_w_up[l], ffn1_w_down[l])
    f2 = _pack_ffn(ffn2_w_gate[l], ffn2_w_up[l], ffn2_w_down[l])
    w_in_p = _pack_w_in(w_in[l])
    w_out_p = w_out[l].astype(BF16)
    w1 = cmp_w1[l]
    w1r = (w1.reshape(2, CMP_BLOCK // CMP_STRIDE, CMP_STRIDE, HEAD_DIM, HEAD_DIM)
           .transpose(0, 2, 3, 1, 4).reshape(2, CMP_STRIDE, HEAD_DIM, 2 * HEAD_DIM).astype(BF16))
    w2 = cmp_w2[l].astype(BF16)
    cmp_bias = _cmp_bias(cmp_pe[l], w1, cmp_b1[l])
    gain = nsa_out_norm[l]

    def trunk_front(x):
        x1, h = _ffn(x, ffn1_norm[l], *f1, mix_norm[l], emit_x=True, norm_dtype=BF16)
        return (x1,) + tuple(_inproj(h, w_in_p))

    def trunk_back(x1, o_nsa, o_hg):
        x2 = _outproj(x1, o_nsa, o_hg, w_out_p)
        return _ffn(x2, ffn2_norm[l], *f2, final_norm, emit_x=False, norm_dtype=F32)[0]

    def kv_rows(kv, lead):
        return kv.reshape(1, *lead, 2, NSA_KV, HEAD_DIM)

    x1, za, za16, zh, zg, kvc, kvs, kvw = trunk_front(x_prompt.reshape(b * t, d))
    ns = t // SEL_BLOCK
    ncp = t // CMP_STRIDE
    msel = _sel_coverage(128, ns, ncp, ncp - 1)
    ckv = _cmp_prompt(za, b, t, w1r, cmp_bias, w2)
    o_nsa = _nsa_prompt(za16, ckv, zg, msel, gain, b, t)
    s0 = jnp.zeros((b, HG_HEADS, HEAD_DIM, HEAD_DIM), F32)
    o_hg, hg_p = _hgrn(zh, hg_lb_logits, hg_out_norm[l], s0, b, t, l, BF16)
    y_prompt = trunk_back(x1, o_nsa, o_hg).reshape(b, t, d)
    p_cmp = kv_rows(kvc, (b, t))
    p_slc = kv_rows(kvs, (b, t))
    p_win = kv_rows(kvw, (b, t))[:, :, t - min(WINDOW, t):]

    x1s, zas, _, zhs, zgs, kvc_s, kvs_s, kvw_s = trunk_front(x_sample.reshape(db * ts, d))
    ncp_s = past_len // CMP_STRIDE
    ns_s = -(-(past_len + ts) // SEL_BLOCK)
    ns_pad = -(-ns_s // 8) * 8
    msel_s = _sel_coverage(ns_pad, ns_s, ncp_s, ncp_s - 1)
    step_keys = SLC_PAGES * PAGE_SIZE
    e_blocks = (jnp.arange(step_keys)[None, :] // SEL_BLOCK == jnp.arange(step_keys // SEL_BLOCK)[:, None]).astype(BF16)
    ckv_s = _cmp_paged(cache_cmp_kv.reshape(-1, HEAD_DIM), l * n_pool, page_table, w1r, cmp_bias, w2)
    wlen = state_win_kv.shape[2]
    win = state_win_kv[l].reshape(db * wlen * KV_GROUPS, HEAD_DIM)
    o_nsa_s = _nsa_sample(zas, zgs, ckv_s, cache_slc_kv.reshape(-1, HEAD_DIM), l * n_pool, win,
                          page_table, msel_s, e_blocks, gain, ts)
    o_hg_s, hg_s = _hgrn(zhs, hg_lb_logits, hg_out_norm[l], state_hgrn[l], db, ts, l, F32)
    y_sample = trunk_back(x1s, o_nsa_s, o_hg_s).reshape(db, ts, d)
    s_cmp = kv_rows(kvc_s, (db, ts))
    s_slc = kv_rows(kvs_s, (db, ts))
    s_win = jnp.concatenate([state_win_kv[l:l + 1, :, ts:].astype(F32), kv_rows(kvw_s, (db, ts))], axis=2)

    return (y_prompt, y_sample, p_cmp, p_slc, p_win, hg_p[None].astype(state_hgrn.dtype),
            s_cmp, s_slc, s_win, hg_s[None].astype(state_hgrn.dtype))
```

```python
import functools

import jax
import jax.numpy as jnp
from jax import lax
from jax.experimental import pallas as pl
from jax.experimental.pallas import tpu as pltpu

F32 = jnp.float32
BF16 = jnp.bfloat16
NEG_INF = float("-inf")

HEAD_DIM = 128
NSA_KV = 2
NSA_GROUP = 4
NSA_HEADS = NSA_KV * NSA_GROUP
NSA_WIDTH = NSA_HEADS * HEAD_DIM
KV_COLS = 2 * NSA_KV * HEAD_DIM
HG_HEADS = 8
HG_WIDTH = HG_HEADS * HEAD_DIM
CMP_BLOCK = 32
CMP_STRIDE = 16
SEL_BLOCK = 64
SEL_SHIFT = 6
SEL_RATIO = SEL_BLOCK // CMP_STRIDE
N_SEL = 16
SEL_SPAN_W = (1.0, 2.0, 2.0, 2.0, 1.0)
WINDOW = 512
PAGE_SIZE = 128
HG_CHUNK = 64
HG_SUB = 16
ATTN_SCALE = HEAD_DIM ** -0.5
HG_SCALE = HEAD_DIM ** -0.5
EPS = 1e-6
EXP2_SCALE = ATTN_SCALE * 1.4426950408889634
MASK_BIAS = 2.0 ** 40

ZA_COLS = NSA_WIDTH + 3 * KV_COLS
ZH_COLS = 4 * HG_WIDTH
ZG_COLS = 128
PROJ_TN = 512
COL_CMP = NSA_WIDTH
COL_SLC = NSA_WIDTH + KV_COLS
COL_WIN = NSA_WIDTH + 2 * KV_COLS

VMEM_LIMIT = 56 * 1024 * 1024


def _cparams(sem):
    return pltpu.CompilerParams(dimension_semantics=sem, vmem_limit_bytes=VMEM_LIMIT)


def _rms(x, g):
    return x * lax.rsqrt(jnp.mean(x * x, axis=-1, keepdims=True) + EPS) * g


def _dot(a, b):
    return jnp.dot(a, b, preferred_element_type=F32)


def _dot_nt(a, b):
    return lax.dot_general(a, b, (((1,), (1,)), ((), ())), preferred_element_type=F32)


def _dot_tn(a, b):
    return lax.dot_general(a, b, (((0,), (0,)), ((), ())), preferred_element_type=F32)


def _split3(x):
    hi = x.astype(BF16)
    r1 = x - hi.astype(F32)
    mid = r1.astype(BF16)
    lo = (r1 - mid.astype(F32)).astype(BF16)
    return hi, mid, lo


def _ffn_kernel(x_ref, g_ref, wg_ref, wu_ref, wd_ref, pg_ref, *rest, emit_x):
    if emit_x:
        xo_ref, no_ref, h_sc, acc_sc = rest
    else:
        no_ref, h_sc, acc_sc = rest
    j = pl.program_id(1)

    @pl.when(j == 0)
    def _():
        h_sc[...] = _rms(x_ref[...], g_ref[...]).astype(BF16)
        acc_sc[...] = jnp.zeros_like(acc_sc)

    h = h_sc[...]
    a = _dot(h, wg_ref[...])
    u = _dot(h, wu_ref[...])
    act = (a * jax.nn.sigmoid(a) * u).astype(BF16)
    acc_sc[...] += _dot(act, wd_ref[...])

    @pl.when(j == pl.num_programs(1) - 1)
    def _():
        xn = x_ref[...] + 0.5 * acc_sc[...]
        if emit_x:
            xo_ref[...] = xn
        no_ref[...] = _rms(xn, pg_ref[...]).astype(no_ref.dtype)


def _ffn(x, g, wg, wu, wd, post_g, *, emit_x, norm_dtype, tf=512):
    n, d = x.shape
    fp = wg.shape[1]
    tm = min(n, 512)
    assert n % tm == 0 and fp % tf == 0
    row = pl.BlockSpec((tm, d), lambda i, j: (i, 0))
    vec = pl.BlockSpec((1, d), lambda i, j: (0, 0))
    out_shape = [jax.ShapeDtypeStruct((n, d), norm_dtype)]
    out_specs = [row]
    if emit_x:
        out_shape = [jax.ShapeDtypeStruct((n, d), F32)] + out_shape
        out_specs = [row] + out_specs
    return pl.pallas_call(
        functools.partial(_ffn_kernel, emit_x=emit_x),
        grid=(n // tm, fp // tf),
        in_specs=[row, vec,
                  pl.BlockSpec((d, tf), lambda i, j: (0, j)),
                  pl.BlockSpec((d, tf), lambda i, j: (0, j)),
                  pl.BlockSpec((tf, d), lambda i, j: (j, 0)),
                  vec],
        out_specs=out_specs,
        out_shape=out_shape,
        scratch_shapes=[pltpu.VMEM((tm, d), BF16), pltpu.VMEM((tm, d), F32)],
        compiler_params=_cparams(("parallel", "arbitrary")),
        name="ffn_half_step",
    )(x, g.reshape(1, d), wg, wu, wd, post_g.reshape(1, d))


def _inproj_kernel(h_ref, w_ref, za_ref, za16_ref, zh_ref, zg_ref, kvc_ref, kvs_ref, kvw_ref, *, na, nh):
    j = pl.program_id(1)
    tm = h_ref.shape[0]

    def product():
        return _dot(h_ref[...], w_ref[...])

    @pl.when(j < COL_CMP // PROJ_TN)
    def _():
        r = product()
        za_ref[...] = r
        za16_ref[...] = r.astype(BF16)

    for col, kv_ref in ((COL_CMP, kvc_ref), (COL_SLC, kvs_ref), (COL_WIN, kvw_ref)):
        @pl.when(j == col // PROJ_TN)
        def _(kv_ref=kv_ref):
            r = product()
            za_ref[...] = r
            za16_ref[...] = r.astype(BF16)
            for ck in range(KV_GROUPS):
                kv_ref[pl.ds(ck, tm, stride=KV_GROUPS), :] = r[:, ck * HEAD_DIM:(ck + 1) * HEAD_DIM]

    @pl.when((j >= na) & (j < na + nh))
    def _():
        zh_ref[...] = product()

    @pl.when(j == na + nh)
    def _():
        zg_ref[...] = product()[:, :ZG_COLS]


def _inproj(h, w):
    n, d = h.shape
    tn = PROJ_TN
    na, nh = ZA_COLS // tn, ZH_COLS // tn
    tm = min(n, 1024)
    assert n % tm == 0 and w.shape[1] == (na + nh + 1) * tn and tn == KV_COLS
    kv_spec = pl.BlockSpec((tm * KV_GROUPS, HEAD_DIM), lambda i, j: (i, 0))
    kv_shape = jax.ShapeDtypeStruct((n * KV_GROUPS, HEAD_DIM), F32)
    return pl.pallas_call(
        functools.partial(_inproj_kernel, na=na, nh=nh),
        grid=(n // tm, na + nh + 1),
        in_specs=[pl.BlockSpec((tm, d), lambda i, j: (i, 0)),
                  pl.BlockSpec((d, tn), lambda i, j: (0, j))],
        out_specs=[pl.BlockSpec((tm, tn), lambda i, j: (i, jnp.minimum(j, na - 1))),
                   pl.BlockSpec((tm, tn), lambda i, j: (i, jnp.minimum(j, na - 1))),
                   pl.BlockSpec((tm, tn), lambda i, j: (i, jnp.clip(j - na, 0, nh - 1))),
                   pl.BlockSpec((tm, ZG_COLS), lambda i, j: (i, 0)),
                   kv_spec, kv_spec, kv_spec],
        out_shape=[jax.ShapeDtypeStruct((n, ZA_COLS), F32),
                   jax.ShapeDtypeStruct((n, ZA_COLS), BF16),
                   jax.ShapeDtypeStruct((n, ZH_COLS), F32),
                   jax.ShapeDtypeStruct((n, ZG_COLS), F32),
                   kv_shape, kv_shape, kv_shape],
        compiler_params=_cparams(("parallel", "arbitrary")),
        name="in_projection",
    )(h, w)


def _outproj_kernel(x_ref, a_ref, b_ref, w_ref, o_ref):
    half = a_ref.shape[1]
    o_ref[...] = (x_ref[...]
                  + _dot(a_ref[...].astype(BF16), w_ref[0:half, :])
                  + _dot(b_ref[...].astype(BF16), w_ref[half:, :]))


def _outproj(x, a, b, w):
    n, d = x.shape
    half = a.shape[1]
    tm = min(n, 512)
    assert n % tm == 0
    return pl.pallas_call(
        _outproj_kernel,
        grid=(n // tm,),
        in_specs=[pl.BlockSpec((tm, d), lambda i: (i, 0)),
                  pl.BlockSpec((tm, half), lambda i: (i, 0)),
                  pl.BlockSpec((tm, half), lambda i: (i, 0)),
                  pl.BlockSpec((2 * half, d), lambda i: (0, 0))],
        out_specs=pl.BlockSpec((tm, d), lambda i: (i, 0)),
        out_shape=jax.ShapeDtypeStruct((n, d), F32),
        compiler_params=_cparams(("parallel",)),
        name="out_projection",
    )(x, a, b, w)


def _cmp_bias_kernel(pe_ref, w1_ref, b1_ref, o_ref):
    for c in range(2):
        pe8 = jnp.broadcast_to(pe_ref[c], (8, pe_ref.shape[2])).astype(BF16)
        o_ref[c] = _dot(pe8, w1_ref[c])[0:1, :] + b1_ref[c]


def _cmp_bias(pe, w1, b1):
    flat = CMP_BLOCK * HEAD_DIM
    return pl.pallas_call(
        _cmp_bias_kernel,
        out_shape=jax.ShapeDtypeStruct((2, 1, HEAD_DIM), F32),
        name="cmp_bias",
    )(pe.reshape(2, 1, flat), w1.reshape(2, flat, HEAD_DIM).astype(BF16), b1.reshape(2, 1, HEAD_DIM))


def _cmp_accumulate(get_rows, w1r_ref, hp_sc, seg0, nss):
    for c in range(2):
        acc = None
        for r in range(CMP_STRIDE):
            xr = jnp.concatenate([get_rows(r, 2 * c + k) for k in range(NSA_KV)], axis=0).astype(BF16)
            d = _dot(xr, w1r_ref[c, r])
            acc = d if acc is None else acc + d
        for k in range(NSA_KV):
            hp_sc[2 * c + k, pl.ds(seg0, nss), :] = acc[k * nss:(k + 1) * nss]


def _cmp_finish(hp_sc, bias_ref, w2_ref, o_ref):
    nseg = hp_sc.shape[1]
    last = lax.broadcasted_iota(jnp.int32, (nseg, 1), 0) == nseg - 1
    for ck in range(2 * NSA_KV):
        c = ck // NSA_KV
        first = hp_sc[ck, :, 0:HEAD_DIM]
        second = pltpu.roll(hp_sc[ck, :, HEAD_DIM:2 * HEAD_DIM], nseg - 1, 0)
        pre = first + second + bias_ref[c]
        out = _dot((pre * jax.nn.sigmoid(pre)).astype(BF16), w2_ref[c])
        o_ref[0, ck] = jnp.where(last, 0.0, out).astype(o_ref.dtype)


def _cmp_prompt_kernel(x0_ref, x1_ref, x2_ref, x3_ref, w1r_ref, bias_ref, w2_ref, o_ref, hp_sc, *, nss):
    j = pl.program_id(1)
    x_refs = (x0_ref, x1_ref, x2_ref, x3_ref)

    def get_rows(r, ck):
        return x_refs[ck][pl.ds(r, nss, stride=CMP_STRIDE), :]

    _cmp_accumulate(get_rows, w1r_ref, hp_sc, pl.multiple_of(j * nss, nss), nss)

    @pl.when(j == pl.num_programs(1) - 1)
    def _():
        _cmp_finish(hp_sc, bias_ref, w2_ref, o_ref)


def _cmp_weight_specs(grid_rank):
    zeros = lambda n: (lambda *a: (0,) * n)
    return [pl.BlockSpec((2, CMP_STRIDE, HEAD_DIM, 2 * HEAD_DIM), zeros(4)),
            pl.BlockSpec((2, 1, HEAD_DIM), zeros(3)),
            pl.BlockSpec((2, HEAD_DIM, HEAD_DIM), zeros(3))]


def _cmp_prompt(za, b, t, w1r, bias, w2):
    rc = min(t, 2048)
    nss = rc // CMP_STRIDE
    nseg = t // CMP_STRIDE
    steps = t // rc
    assert t % rc == 0
    return pl.pallas_call(
        functools.partial(_cmp_prompt_kernel, nss=nss),
        grid=(b, steps),
        in_specs=[pl.BlockSpec((rc, HEAD_DIM), functools.partial(lambda i, j, ck: (i * steps + j, COL_CMP // HEAD_DIM + ck), ck=ck))
                  for ck in range(2 * NSA_KV)] + _cmp_weight_specs(2),
        out_specs=pl.BlockSpec((1, 2 * NSA_KV, nseg, HEAD_DIM), lambda i, j: (i, 0, 0, 0)),
        out_shape=jax.ShapeDtypeStruct((b, 2 * NSA_KV, nseg, HEAD_DIM), BF16),
        scratch_shapes=[pltpu.VMEM((2 * NSA_KV, nseg, 2 * HEAD_DIM), F32)],
        compiler_params=_cparams(("parallel", "arbitrary")),
        name="cmp_mlp_prompt",
    )(za, za, za, za, w1r, bias, w2)


CMP_PAGES = 16


KV_GROUPS = 2 * NSA_KV
PAGE_ROWS = PAGE_SIZE * KV_GROUPS


def _cmp_paged_kernel(pt_ref, *refs):
    pages = refs[:CMP_PAGES]
    perm_ref, w1p_ref, bias_ref, w2_ref, o_ref, hp_sc, x_sc = refs[CMP_PAGES:]
    j = pl.program_id(1)
    spp = PAGE_SIZE // CMP_STRIDE
    nss = CMP_PAGES * spp
    seg0 = pl.multiple_of(j * nss, nss)
    for c in range(2):
        for g, pg in enumerate(pages):
            xc = jnp.concatenate([pg[pl.ds(NSA_KV * c + k, PAGE_SIZE, stride=KV_GROUPS), :] for k in range(NSA_KV)],
                                 axis=1).astype(BF16)
            xp = _dot(perm_ref[...], xc)
            for r in range(CMP_STRIDE):
                for k in range(NSA_KV):
                    x_sc[c, r, k, g * spp:(g + 1) * spp, :] = xp[r * spp:(r + 1) * spp, k * HEAD_DIM:(k + 1) * HEAD_DIM]
    for c in range(2):
        for k in range(NSA_KV):
            lhs = jnp.concatenate([x_sc[c, r, k] for r in range(CMP_STRIDE)], axis=1).astype(BF16)
            hp_sc[2 * c + k, pl.ds(seg0, nss), :] = _dot(lhs, w1p_ref[c])

    @pl.when(j == pl.num_programs(1) - 1)
    def _():
        _cmp_finish(hp_sc, bias_ref, w2_ref, o_ref)


def _page_specs(n, page0):
    def spec(g):
        return pl.BlockSpec((PAGE_ROWS, HEAD_DIM), lambda i, j, pt: (page0 + pt[i, j * n + g], 0))
    return [spec(g) for g in range(n)]


def _cmp_paged(cache, page0, page_table, w1r, bias, w2):
    db, n_pages = page_table.shape
    assert n_pages % CMP_PAGES == 0 and PAGE_SIZE == HEAD_DIM
    nseg = n_pages * PAGE_SIZE // CMP_STRIDE
    spp = PAGE_SIZE // CMP_STRIDE
    nss = CMP_PAGES * spp
    i = jnp.arange(PAGE_SIZE)
    perm = (i[None, :] == ((i % spp) * CMP_STRIDE + i // spp)[:, None]).astype(BF16)
    w1p = w1r.reshape(2, CMP_STRIDE * HEAD_DIM, 2 * HEAD_DIM)
    zeros = lambda n: (lambda i, j, pt: (0,) * n)
    grid_spec = pltpu.PrefetchScalarGridSpec(
        num_scalar_prefetch=1,
        grid=(db, n_pages // CMP_PAGES),
        in_specs=_page_specs(CMP_PAGES, page0) + [
            pl.BlockSpec((PAGE_SIZE, PAGE_SIZE), zeros(2)),
            pl.BlockSpec((2, CMP_STRIDE * HEAD_DIM, 2 * HEAD_DIM), zeros(3)),
            pl.BlockSpec((2, 1, HEAD_DIM), zeros(3)),
            pl.BlockSpec((2, HEAD_DIM, HEAD_DIM), zeros(3))],
        out_specs=pl.BlockSpec((1, 2 * NSA_KV, nseg, HEAD_DIM), lambda i, j, pt: (i, 0, 0, 0)),
        scratch_shapes=[pltpu.VMEM((2 * NSA_KV, nseg, 2 * HEAD_DIM), F32),
                        pltpu.VMEM((2, CMP_STRIDE, NSA_KV, nss, HEAD_DIM), F32)])
    return pl.pallas_call(
        _cmp_paged_kernel,
        grid_spec=grid_spec,
        out_shape=jax.ShapeDtypeStruct((db, 2 * NSA_KV, nseg, HEAD_DIM), BF16),
        compiler_params=_cparams(("parallel", "arbitrary")),
        name="cmp_mlp_paged",
    )(page_table, *([cache] * CMP_PAGES), perm, w1p, bias, w2)


def _softmax0(s):
    m = jnp.max(s, axis=0, keepdims=True)
    m = jnp.where(m > NEG_INF, m, 0.0)
    e = jnp.exp2((s - m) * EXP2_SCALE)
    l = jnp.sum(e, axis=0, keepdims=True)
    return e * (1.0 / jnp.maximum(l, 1e-30))


def _sel_importance_t(msel, u):
    hi, mid, lo = _split3(u)
    return _dot(msel, hi) + _dot(msel, mid) + _dot(msel, lo)


def _topk_t(pslc, tpos, nsel):
    nsp = pslc.shape[0]
    jio = lax.broadcasted_iota(jnp.int32, pslc.shape, 0)
    jt = lax.shift_right_logical(tpos, SEL_SHIFT)
    forced = (jio == 0) | (jio == jt) | (jio == jt - 1)
    score0 = jnp.where(forced, jnp.inf, jnp.where(jio <= jt, pslc, NEG_INF))
    score = score0
    for _ in range(nsel):
        m = jnp.max(score, axis=0, keepdims=True)
        idx = jnp.min(jnp.where(score == m, jio, nsp), axis=0, keepdims=True)
        score = jnp.where(jio == idx, NEG_INF, score)
    return jnp.where(score != score0, 1.0, 0.0)


def _nsa_prompt_kernel(q_ref, ks_ref, vs_ref, kw_ref, vw_ref, eb_ref, ck_ref, cv_ref, zg_ref, msel_ref, gain_ref,
                       o_ref, gate_sc, s_sc, m_sc, l_sc, acc_sc, *, tq, tk, nsel, chains):
    k = pl.program_id(1)
    i = pl.program_id(2)
    t0 = i * tq
    r = NSA_GROUP * tq
    q = q_ref[...]
    qr = jnp.concatenate([q[:, g * HEAD_DIM:(g + 1) * HEAD_DIM] for g in range(NSA_GROUP)], axis=0)
    tpos1 = t0 + lax.broadcasted_iota(jnp.int32, (1, tq), 1)
    tpos = jnp.concatenate([tpos1] * NSA_GROUP, axis=1)

    ncp = ck_ref.shape[2]
    wlen = WINDOW + tq
    ws = pl.multiple_of(jnp.maximum(t0 - WINDOW, 0), tq)
    s_cmp = _dot_nt(ck_ref[0, 0], qr)
    s_win = _dot_nt(kw_ref[pl.ds(ws, wlen), :], qr)

    cend = lax.broadcasted_iota(jnp.int32, (ncp, 1), 0) * CMP_STRIDE + (CMP_BLOCK - 1)
    p = _softmax0(jnp.where(cend <= tpos, s_cmp, NEG_INF))
    imp = p[:, 0:tq]
    for g in range(1, NSA_GROUP):
        imp = imp + p[:, g * tq:(g + 1) * tq]
    pslc = _sel_importance_t(msel_ref[...], imp)
    o_cmp = _dot_tn(cv_ref[0, 0], p.astype(BF16))

    d = tpos - (ws + lax.broadcasted_iota(jnp.int32, (wlen, 1), 0))
    in_window = lax.bitcast_convert_type(d, jnp.uint32) <= WINDOW
    sw = jnp.where(in_window, s_win, NEG_INF)
    ew = jnp.exp2((sw - jnp.max(sw, axis=0, keepdims=True)) * EXP2_SCALE)
    lw = jnp.sum(ew, axis=0, keepdims=True)
    o_win = _dot_tn(vw_ref[pl.ds(ws, wlen), :], ew.astype(BF16)) * (1.0 / jnp.maximum(lw, 1e-30))

    sel = _topk_t(pslc, tpos1, nsel)

    selm1 = (sel - 1.0).T.astype(BF16)
    q_aug = jnp.concatenate([qr, jnp.concatenate([selm1] * NSA_GROUP, axis=0)], axis=1)
    ones8 = jnp.ones((8, tk), BF16)
    rc = r // chains

    def k_tile(kt):
        off = pl.multiple_of(kt * tk, tk)
        return jnp.concatenate([ks_ref[pl.ds(off, tk), :], eb_ref[pl.ds(off, tk), :]], axis=1)

    def put_scores(c, k_aug):
        s_sc[c] = _dot_nt(k_aug, q_aug[c * rc:(c + 1) * rc])

    def softmax_tile(c, kt, m, causal):
        s = s_sc[c]
        if causal:
            visible = kt * tk + lax.broadcasted_iota(jnp.int32, (tk, 1), 0) <= tpos[:, c * rc:(c + 1) * rc]
            s = jnp.where(visible, s, NEG_INF)
        m_new = jnp.maximum(m, jnp.max(s, axis=0, keepdims=True))
        s = s_sc[c]
        if causal:
            s = jnp.where(visible, s, NEG_INF)
        return m_new, jnp.exp2((s - m_new) * EXP2_SCALE).astype(BF16)

    def accumulate(c, m, m_new, e, v):
        alpha = jnp.exp2((m - m_new) * EXP2_SCALE)
        l_sc[c] = alpha * l_sc[c] + _dot(ones8, e)[0:1, :]
        acc_sc[c] = alpha * acc_sc[c] + _dot_tn(v, e)
        m_sc[c] = m_new

    def step(kt):
        v = vs_ref[pl.ds(pl.multiple_of(kt * tk, tk), tk), :]
        k_next = k_tile(kt + 1)
        pending = None
        for c in range(chains):
            m = m_sc[c]
            m_new, e = softmax_tile(c, kt, m, False)
            if pending is not None:
                accumulate(*pending)
            put_scores(c, k_next)
            pending = (c, m, m_new, e, v)
        accumulate(*pending)

    unroll = 4

    def step_group(gi, carry):
        for u in range(unroll):
            step(unroll * gi + u)
        return carry

    n_full = t0 // tk
    k0 = k_tile(0)
    for c in range(chains):
        put_scores(c, k0)
        m_sc[c] = jnp.full((1, rc), NEG_INF, F32)
        l_sc[c] = jnp.zeros((1, rc), F32)
        acc_sc[c] = jnp.zeros((HEAD_DIM, rc), F32)
    lax.fori_loop(0, n_full // unroll, step_group, 0)
    done = (n_full // unroll) * unroll
    width = unroll // 2
    while width >= 1:
        @pl.when((n_full & width) != 0)
        def _(width=width, done=done):
            for u in range(width):
                step(done + u)
        done = done + (n_full & width)
        width //= 2

    v = vs_ref[pl.ds(pl.multiple_of(n_full * tk, tk), tk), :]
    for c in range(chains):
        m = m_sc[c]
        m_new, e = softmax_tile(c, n_full, m, True)
        accumulate(c, m, m_new, e, v)
    o_slc = jnp.concatenate([acc_sc[c] * (1.0 / jnp.maximum(l_sc[c], 1e-30)) for c in range(chains)], axis=1)

    gate_sc[...] = jax.nn.sigmoid(zg_ref[...].T[0:gate_sc.shape[0], :])

    def gate_row(branch):
        return jnp.concatenate(
            [gate_sc[pl.ds(branch * NSA_HEADS + k * NSA_GROUP + g, 1), :] for g in range(NSA_GROUP)], axis=1)

    o = gate_row(0) * o_cmp + gate_row(1) * o_slc + gate_row(2) * o_win
    o = o * lax.rsqrt(jnp.mean(o * o, axis=0, keepdims=True) + EPS)
    for g in range(NSA_GROUP):
        og = o[:, g * tq:(g + 1) * tq] * gain_ref[k * NSA_GROUP + g]
        o_ref[:, g * HEAD_DIM:(g + 1) * HEAD_DIM] = og.T.astype(o_ref.dtype)


NSA_TQ = 256
NSA_CHAINS = 2


def _nsa_prompt(za16, ckv, zg, msel, gain, b, t):
    tq = NSA_TQ
    tk = min(t, 512)
    gain_b = jnp.broadcast_to(gain[:, :, None], (NSA_HEADS, HEAD_DIM, tq))
    nq = t // tq
    ncp = ckv.shape[2]
    nsp = msel.shape[0]
    qw = NSA_GROUP * HEAD_DIM
    assert t % tk == 0 and tk % tq == 0 and t >= WINDOW + tq and nsp == 128
    col = lambda base, kk: (base // HEAD_DIM) + kk
    block_bias = jnp.where(jnp.arange(t)[:, None] // SEL_BLOCK == jnp.arange(nsp)[None, :], MASK_BIAS, 0.0).astype(BF16)

    def seq_spec(base):
        return pl.BlockSpec((t, HEAD_DIM), lambda bi, k, i: (bi, col(base, k)))

    return pl.pallas_call(
        functools.partial(_nsa_prompt_kernel, tq=tq, tk=tk, nsel=min(N_SEL, t // SEL_BLOCK), chains=NSA_CHAINS),
        grid=(b, NSA_KV, nq),
        in_specs=[pl.BlockSpec((tq, qw), lambda bi, k, i: (bi * nq + i, k)),
                  seq_spec(COL_SLC), seq_spec(COL_SLC + NSA_KV * HEAD_DIM),
                  seq_spec(COL_WIN), seq_spec(COL_WIN + NSA_KV * HEAD_DIM),
                  pl.BlockSpec((t, nsp), lambda bi, k, i: (0, 0)),
                  pl.BlockSpec((1, 1, ncp, HEAD_DIM), lambda bi, k, i: (bi, k, 0, 0)),
                  pl.BlockSpec((1, 1, ncp, HEAD_DIM), lambda bi, k, i: (bi, NSA_KV + k, 0, 0)),
                  pl.BlockSpec((tq, ZG_COLS), lambda bi, k, i: (bi * nq + i, 0)),
                  pl.BlockSpec((nsp, ncp), lambda bi, k, i: (0, 0)),
                  pl.BlockSpec((NSA_HEADS, HEAD_DIM, tq), lambda bi, k, i: (0, 0, 0))],
        out_specs=pl.BlockSpec((tq, qw), lambda bi, k, i: (bi * nq + i, k)),
        out_shape=jax.ShapeDtypeStruct((b * t, NSA_WIDTH), BF16),
        scratch_shapes=[pltpu.VMEM((32, tq), F32),
                        pltpu.VMEM((NSA_CHAINS, tk, NSA_GROUP * tq // NSA_CHAINS), F32),
                        pltpu.VMEM((NSA_CHAINS, 1, NSA_GROUP * tq // NSA_CHAINS), F32),
                        pltpu.VMEM((NSA_CHAINS, 1, NSA_GROUP * tq // NSA_CHAINS), F32),
                        pltpu.VMEM((NSA_CHAINS, HEAD_DIM, NSA_GROUP * tq // NSA_CHAINS), F32)],
        compiler_params=_cparams(("parallel", "parallel", "arbitrary")),
        name="nsa_prompt",
    )(za16, za16, za16, za16, za16, block_bias, ckv, ckv, zg, msel, gain_b)


SLC_PAGES = 16


def _nsa_sample_kernel(pt_ref, *refs, ts, past_len):
    pages = refs[:SLC_PAGES]
    (z_ref, zg_ref, ckv_ref, win_ref, msel_ref, e_ref, gain_ref,
     o_ref, sel_sc, m_sc, l_sc, acc_sc, ocmp_sc) = refs[SLC_PAGES:]
    j = pl.program_id(1)
    rows = NSA_GROUP * ts
    lanes = 128
    step_keys = SLC_PAGES * PAGE_SIZE
    step_blocks = step_keys // SEL_BLOCK
    z = z_ref[...]

    def q_rows(k):
        return jnp.concatenate([z[:, (k * NSA_GROUP + g) * HEAD_DIM:(k * NSA_GROUP + g + 1) * HEAD_DIM]
                                for g in range(NSA_GROUP)], axis=0).astype(BF16)

    @pl.when(j == 0)
    def _():
        lane = lax.broadcasted_iota(jnp.int32, (1, lanes), 1)
        tpos = past_len + jnp.bitwise_and(lane, ts - 1)
        s = None
        for k in range(NSA_KV):
            pads = (k * rows, lanes - (k + 1) * rows)
            parts = [jnp.zeros((pads[0], HEAD_DIM), BF16), q_rows(k), jnp.zeros((pads[1], HEAD_DIM), BF16)]
            qp = jnp.concatenate([x for x in parts if x.shape[0] > 0], axis=0)
            sk = _dot_nt(ckv_ref[0, k], qp)
            s = sk if s is None else s + sk
        ncp = s.shape[0]
        cend = lax.broadcasted_iota(jnp.int32, (ncp, 1), 0) * CMP_STRIDE + (CMP_BLOCK - 1)
        p = _softmax0(jnp.where(cend <= tpos, s, NEG_INF))
        p16 = p.astype(BF16)
        for k in range(NSA_KV):
            ocmp_sc[k] = _dot_tn(ckv_ref[0, NSA_KV + k], p16).T[k * rows:(k + 1) * rows, :]
            m_sc[k] = jnp.full((rows, 1), NEG_INF, F32)
            l_sc[k] = jnp.zeros((rows, 1), F32)
            acc_sc[k] = jnp.zeros((rows, HEAD_DIM), F32)
        u = _sel_importance_t(msel_ref[...], p)
        pslc = u
        for g in range(1, NSA_GROUP):
            pslc = pslc + pltpu.roll(u, lanes - g * ts, 1)
        sel_sc[...] = _topk_t(pslc, tpos, N_SEL)

    def online(k, s, v):
        m_old = m_sc[k]
        m_new = jnp.maximum(m_old, jnp.max(s, axis=1, keepdims=True))
        alpha = jnp.exp2((m_old - m_new) * EXP2_SCALE)
        e = jnp.exp2((s - m_new) * EXP2_SCALE)
        l_sc[k] = alpha * l_sc[k] + jnp.sum(e, axis=1, keepdims=True)
        acc_sc[k] = alpha * acc_sc[k] + _dot(e.astype(BF16), v)
        m_sc[k] = m_new

    def kv_rows(ref, ck, n):
        return ref[pl.ds(ck, n, stride=KV_GROUPS), :].astype(BF16)

    selj = sel_sc[pl.ds(pl.multiple_of(j * step_blocks, step_blocks), step_blocks), :]
    mask_all = _dot_tn(selj.astype(BF16), e_ref[...])
    staged = []
    for k in range(NSA_KV):
        kt = jnp.concatenate([kv_rows(pg, k, PAGE_SIZE) for pg in pages], axis=0)
        mask = jnp.concatenate([mask_all[k * rows:k * rows + ts]] * NSA_GROUP, axis=0) > 0.5
        staged.append(jnp.where(mask, _dot_nt(q_rows(k), kt), NEG_INF))
    for k in range(NSA_KV):
        vt = jnp.concatenate([kv_rows(pg, NSA_KV + k, PAGE_SIZE) for pg in pages], axis=0)
        online(k, staged[k], vt)

    @pl.when(j == pl.num_programs(1) - 1)
    def _():
        row_t = jnp.bitwise_and(lax.broadcasted_iota(jnp.int32, (rows, 1), 0), ts - 1)
        new_i = lax.broadcasted_iota(jnp.int32, (1, 2 * ts), 1)
        new_ok = (new_i < ts) & (new_i <= row_t)
        pad = jnp.zeros((ts, HEAD_DIM), F32)

        def new_rows(col):
            return jnp.concatenate([z[:, col:col + HEAD_DIM], pad], axis=0).astype(BF16)

        wlen = win_ref.shape[0] // KV_GROUPS
        wpos = past_len - wlen + lax.broadcasted_iota(jnp.int32, (1, wlen), 1)
        dw = (past_len + row_t) - wpos
        win_ok = (dw >= 0) & (dw <= WINDOW) & (wpos >= 0)
        zg = zg_ref[...]
        raw = []
        for k in range(NSA_KV):
            q = q_rows(k)
            raw.append((_dot_nt(q, new_rows(COL_SLC + k * HEAD_DIM)),
                        _dot_nt(q, kv_rows(win_ref, k, wlen)),
                        _dot_nt(q, new_rows(COL_WIN + k * HEAD_DIM))))
        for k in range(NSA_KV):
            s_new, s_buf, s_wnew = raw[k]
            online(k, jnp.where(new_ok, s_new, NEG_INF), new_rows(COL_SLC + (NSA_KV + k) * HEAD_DIM))
            o_slc = acc_sc[k] * (1.0 / jnp.maximum(l_sc[k], 1e-30))
            vw = kv_rows(win_ref, NSA_KV + k, wlen)
            s1 = jnp.where(win_ok, s_buf, NEG_INF)
            s2 = jnp.where(new_ok, s_wnew, NEG_INF)
            mw = jnp.maximum(jnp.max(s1, axis=1, keepdims=True), jnp.max(s2, axis=1, keepdims=True))
            e1 = jnp.exp2((s1 - mw) * EXP2_SCALE)
            e2 = jnp.exp2((s2 - mw) * EXP2_SCALE)
            lw = jnp.sum(e1, axis=1, keepdims=True) + jnp.sum(e2, axis=1, keepdims=True)
            o_win = (_dot(e1.astype(BF16), vw)
                     + _dot(e2.astype(BF16), new_rows(COL_WIN + (NSA_KV + k) * HEAD_DIM))) * (1.0 / jnp.maximum(lw, 1e-30))

            def gate(branch):
                c0 = branch * NSA_HEADS + k * NSA_GROUP
                return jnp.concatenate([jax.nn.sigmoid(zg[:, c0 + g:c0 + g + 1]) for g in range(NSA_GROUP)], axis=0)

            o = gate(0) * ocmp_sc[k] + gate(1) * o_slc + gate(2) * o_win
            o = o * lax.rsqrt(jnp.mean(o * o, axis=1, keepdims=True) + EPS)
            for g in range(NSA_GROUP):
                h = k * NSA_GROUP + g
                o_ref[:, h * HEAD_DIM:(h + 1) * HEAD_DIM] = o[g * ts:(g + 1) * ts, :] * gain_ref[h:h + 1, :]


def _nsa_sample(za, zg, ckv, cache_slc, page0, win, page_table, msel, e_blocks, gain, ts):
    db, n_pages = page_table.shape
    past_len = n_pages * PAGE_SIZE
    ncp = ckv.shape[2]
    nsp = msel.shape[0]
    wrows = win.shape[0] // db
    rows = NSA_GROUP * ts
    step_keys = SLC_PAGES * PAGE_SIZE
    assert n_pages % SLC_PAGES == 0 and ts & (ts - 1) == 0 and ts % 8 == 0 and rows <= 128
    assert past_len % SEL_BLOCK == 0 and ts <= SEL_BLOCK and ts < CMP_STRIDE
    zeros = lambda n: (lambda i, j, pt: (0,) * n)
    grid_spec = pltpu.PrefetchScalarGridSpec(
        num_scalar_prefetch=1,
        grid=(db, n_pages // SLC_PAGES),
        in_specs=_page_specs(SLC_PAGES, page0) + [
            pl.BlockSpec((ts, ZA_COLS), lambda i, j, pt: (i, 0)),
            pl.BlockSpec((ts, ZG_COLS), lambda i, j, pt: (i, 0)),
            pl.BlockSpec((1, 2 * NSA_KV, ncp, HEAD_DIM), lambda i, j, pt: (i, 0, 0, 0)),
            pl.BlockSpec((wrows, HEAD_DIM), lambda i, j, pt: (i, 0)),
            pl.BlockSpec((nsp, ncp), zeros(2)),
            pl.BlockSpec((step_keys // SEL_BLOCK, step_keys), zeros(2)),
            pl.BlockSpec((NSA_HEADS, HEAD_DIM), zeros(2))],
        out_specs=pl.BlockSpec((ts, NSA_WIDTH), lambda i, j, pt: (i, 0)),
        scratch_shapes=[pltpu.VMEM((nsp, 128), F32),
                        pltpu.VMEM((NSA_KV, rows, 1), F32),
                        pltpu.VMEM((NSA_KV, rows, 1), F32),
                        pltpu.VMEM((NSA_KV, rows, HEAD_DIM), F32),
                        pltpu.VMEM((NSA_KV, rows, HEAD_DIM), F32)])
    return pl.pallas_call(
        functools.partial(_nsa_sample_kernel, ts=ts, past_len=past_len),
        grid_spec=grid_spec,
        out_shape=jax.ShapeDtypeStruct((db * ts, NSA_WIDTH), F32),
        compiler_params=_cparams(("parallel", "arbitrary")),
        name="nsa_sample",
    )(page_table, *([cache_slc] * SLC_PAGES), za, zg, ckv, win, msel, e_blocks, gain)


def _cumsum0(x):
    n = x.shape[0]
    row = lax.broadcasted_iota(jnp.int32, (n, 1), 0)
    sh = 1
    while sh < n:
        x = x + jnp.where(row >= sh, pltpu.roll(x, sh, 0), 0.0)
        sh *= 2
    return x


HG_PAR = 4


def _hgrn_heads_chunk(heads, *, c, sb):
    row8 = lax.broadcasted_iota(jnp.int32, (8, 1), 0)
    pre = []
    for hq, hf, v, gt, lb, gain, st in heads:
        fg = lb + (1.0 - lb) * jax.nn.sigmoid(hf)
        kk = 1.0 - fg
        bc = _cumsum0(jnp.log2(fg))
        qh = hq * jax.nn.sigmoid(hq) * HG_SCALE
        pre.append((kk, bc, qh, v.astype(BF16)))
    o_inter = [_dot_nt((qh * jnp.exp2(bc)).astype(BF16), head[6].astype(BF16))
               for (kk, bc, qh, v16), head in zip(pre, heads)]
    pieces = [[] for _ in heads]
    for si in range(c // sb):
        lo = si * sb
        scores = []
        if si > 0:
            for kk, bc, qh, v16 in pre:
                bcr = bc[lo - 1:lo]
                qs = (qh[lo:lo + sb] * jnp.exp2(bc[lo:lo + sb] - bcr)).astype(BF16)
                kp = (kk[0:lo] * jnp.exp2(bcr - bc[0:lo])).astype(BF16)
                scores.append(_dot_nt(qs, kp).astype(BF16))
        for hi, ((kk, bc, qh, v16), head) in enumerate(zip(pre, heads)):
            v = head[2]
            od = [jnp.zeros((8, HEAD_DIM), F32) for _ in range(sb // 8)]
            for s in range(sb):
                ks, vs_, bs = kk[lo + s:lo + s + 1], v[lo + s:lo + s + 1], bc[lo + s:lo + s + 1]
                for p in range(s // 8, sb // 8):
                    r0 = lo + 8 * p
                    diff = bc[r0:r0 + 8] - bs
                    if p == s // 8:
                        diff = jnp.where(row8 >= s % 8, diff, NEG_INF)
                    a = jnp.sum(qh[r0:r0 + 8] * ks * jnp.exp2(diff), axis=1, keepdims=True)
                    od[p] = od[p] + a * vs_
            od = od[0] if len(od) == 1 else jnp.concatenate(od, axis=0)
            if si > 0:
                od = od + _dot(scores[hi], v16[0:lo])
            pieces[hi].append(od)
    out = []
    for (kk, bc, qh, v16), head, oi, pc in zip(pre, heads, o_inter, pieces):
        gt, gain, st = head[3], head[5], head[6]
        o = oi + (pc[0] if len(pc) == 1 else jnp.concatenate(pc, axis=0))
        bl = bc[c - 1:c]
        st_new = st * jnp.exp2(bl) + _dot_tn(v16, (kk * jnp.exp2(bl - bc)).astype(BF16))
        out.append((_rms(o, gain) * (gt * jax.nn.sigmoid(gt)), st_new))
    return out


def _hgrn_kernel(q_ref, f_ref, i_ref, g_ref, lbl_ref, on_ref, s0_ref, o_ref, so_ref, st_sc, *, c, sb, layer):
    j = pl.program_id(2)
    nh = st_sc.shape[0]

    @pl.when(j == 0)
    def _():
        for h in range(nh):
            st_sc[h] = s0_ref[0, h].T

    lbl = lbl_ref[...]
    e = jnp.exp(lbl - jnp.max(lbl, axis=0, keepdims=True))
    lb = jnp.sum(e[0:layer + 1], axis=0, keepdims=True) / jnp.sum(e, axis=0, keepdims=True)
    gain = on_ref[...]

    def chunk(ci, carry):
        rows = pl.ds(pl.multiple_of(ci * c, c), c)
        col = lambda h: slice(h * HEAD_DIM, (h + 1) * HEAD_DIM)
        heads = [(q_ref[rows, col(h)], f_ref[rows, col(h)], i_ref[rows, col(h)], g_ref[rows, col(h)],
                  lb[:, col(h)], gain[:, col(h)], st_sc[h]) for h in range(nh)]
        for h, (y, st_new) in enumerate(_hgrn_heads_chunk(heads, c=c, sb=sb)):
            st_sc[h] = st_new
            o_ref[rows, col(h)] = y.astype(o_ref.dtype)
        return carry

    lax.fori_loop(0, q_ref.shape[0] // c, chunk, 0)

    @pl.when(j == pl.num_programs(2) - 1)
    def _():
        for h in range(nh):
            so_ref[0, h] = st_sc[h].T


def _hgrn(zh, lb_logits, out_norm, s0, b, t, layer, out_dtype):
    c = HG_CHUNK if t % HG_CHUNK == 0 else t
    sb = min(HG_SUB, c)
    tc = min(t, 512)
    nt = t // tc
    hw = HG_PAR * HEAD_DIM
    ng = HG_HEADS // HG_PAR
    assert t % tc == 0 and tc % c == 0 and c % sb == 0 and sb % 8 == 0 and HG_HEADS % HG_PAR == 0

    def piece(p):
        return pl.BlockSpec((tc, hw), lambda bi, h, j: (bi * nt + j, p * ng + h))

    st_spec = pl.BlockSpec((1, HG_PAR, HEAD_DIM, HEAD_DIM), lambda bi, h, j: (bi, h, 0, 0))
    nl = lb_logits.shape[0]
    return pl.pallas_call(
        functools.partial(_hgrn_kernel, c=c, sb=sb, layer=layer),
        grid=(b, ng, nt),
        in_specs=[piece(0), piece(1), piece(2), piece(3),
                  pl.BlockSpec((nl, hw), lambda bi, h, j: (0, h)),
                  pl.BlockSpec((1, hw), lambda bi, h, j: (0, h)),
                  st_spec],
        out_specs=[pl.BlockSpec((tc, hw), lambda bi, h, j: (bi * nt + j, h)), st_spec],
        out_shape=[jax.ShapeDtypeStruct((b * t, HG_WIDTH), out_dtype),
                   jax.ShapeDtypeStruct((b, HG_HEADS, HEAD_DIM, HEAD_DIM), F32)],
        scratch_shapes=[pltpu.VMEM((HG_PAR, HEAD_DIM, HEAD_DIM), F32)],
        compiler_params=_cparams(("parallel", "parallel", "arbitrary")),
        name="hgrn2",
    )(zh, zh, zh, zh, lb_logits, out_norm.reshape(1, HG_WIDTH), s0)


def _sel_coverage(ns_pad, ns, ncp, nc):
    j = jnp.arange(ns_pad)[:, None]
    i = jnp.arange(ncp)[None, :]
    d = i - (SEL_RATIO * j - 1)
    w = jnp.zeros((ns_pad, ncp), F32)
    for dd, ww in enumerate(SEL_SPAN_W):
        w = jnp.where(d == dd, ww, w)
    return jnp.where((j < ns) & (i < nc), w, 0.0).astype(BF16)


def _pack_w_in(w):
    gates = w[:, ZA_COLS:ZA_COLS + 3 * NSA_HEADS]
    return jnp.concatenate([w[:, :ZA_COLS], w[:, ZA_COLS + 3 * NSA_HEADS:],
                            jnp.pad(gates, ((0, 0), (0, PROJ_TN - 3 * NSA_HEADS)))], axis=1).astype(BF16)


def _pack_ffn(wg, wu, wd, tf=512):
    f = wg.shape[1]
    fp = -(-f // tf) * tf
    padc = ((0, 0), (0, fp - f))
    return (jnp.pad(wg.astype(BF16), padc), jnp.pad(wu.astype(BF16), padc),
            jnp.pad(wd.astype(BF16), ((0, fp - f), (0, 0))))


def kernel(x_prompt, x_sample, cache_cmp_kv, cache_slc_kv, state_win_kv, state_hgrn, page_table, ffn1_norm, ffn1_w_gate, ffn1_w_up, ffn1_w_down, mix_norm, w_in, cmp_pe, cmp_w1, cmp_b1, cmp_w2, nsa_out_norm, hg_lb_logits, hg_out_norm, w_out, ffn2_norm, ffn2_w_gate, ffn2_w_up, ffn2_w_down, final_norm):
    depth = w_in.shape[0]
    assert depth == 1, "single-layer trunk"
    l = 0
    b, t, d = x_prompt.shape
    db, ts, _ = x_sample.shape
    n_pool = cache_cmp_kv.shape[1]
    n_pages = page_table.shape[1]
    past_len = n_pages * PAGE_SIZE
    assert state_win_kv.shape[2] == min(WINDOW, past_len) and t % SEL_BLOCK == 0

    f1 = _pack_ffn(ffn1_w_gate[l], ffn1_w_up[l], ffn1_w_down[l])
    f2 = _pack_ffn(ffn2_w_gate[l], ffn2_w_up[l], ffn2_w_down[l])
    w_in_p = _pack_w_in(w_in[l])
    w_out_p = w_out[l].astype(BF16)
    w1 = cmp_w1[l]
    w1r = (w1.reshape(2, CMP_BLOCK // CMP_STRIDE, CMP_STRIDE, HEAD_DIM, HEAD_DIM)
           .transpose(0, 2, 3, 1, 4).reshape(2, CMP_STRIDE, HEAD_DIM, 2 * HEAD_DIM).astype(BF16))
    w2 = cmp_w2[l].astype(BF16)
    cmp_bias = _cmp_bias(cmp_pe[l], w1, cmp_b1[l])
    gain = nsa_out_norm[l]

    def trunk_front(x):
        x1, h = _ffn(x, ffn1_norm[l], *f1, mix_norm[l], emit_x=True, norm_dtype=BF16)
        return (x1,) + tuple(_inproj(h, w_in_p))

    def trunk_back(x1, o_nsa, o_hg):
        x2 = _outproj(x1, o_nsa, o_hg, w_out_p)
        return _ffn(x2, ffn2_norm[l], *f2, final_norm, emit_x=False, norm_dtype=F32)[0]

    def kv_rows(kv, lead):
        return kv.reshape(1, *lead, 2, NSA_KV, HEAD_DIM)

    x1, za, za16, zh, zg, kvc, kvs, kvw = trunk_front(x_prompt.reshape(b * t, d))
    ns = t // SEL_BLOCK
    ncp = t // CMP_STRIDE
    msel = _sel_coverage(128, ns, ncp, ncp - 1)
    ckv = _cmp_prompt(za, b, t, w1r, cmp_bias, w2)
    o_nsa = _nsa_prompt(za16, ckv, zg, msel, gain, b, t)
    s0 = jnp.zeros((b, HG_HEADS, HEAD_DIM, HEAD_DIM), F32)
    o_hg, hg_p = _hgrn(zh, hg_lb_logits, hg_out_norm[l], s0, b, t, l, BF16)
    y_prompt = trunk_back(x1, o_nsa, o_hg).reshape(b, t, d)
    p_cmp = kv_rows(kvc, (b, t))
    p_slc = kv_rows(kvs, (b, t))
    p_win = kv_rows(kvw, (b, t))[:, :, t - min(WINDOW, t):]

    x1s, zas, _, zhs, zgs, kvc_s, kvs_s, kvw_s = trunk_front(x_sample.reshape(db * ts, d))
    ncp_s = past_len // CMP_STRIDE
    ns_s = -(-(past_len + ts) // SEL_BLOCK)
    ns_pad = -(-ns_s // 8) * 8
    msel_s = _sel_coverage(ns_pad, ns_s, ncp_s, ncp_s - 1)
    step_keys = SLC_PAGES * PAGE_SIZE
    e_blocks = (jnp.arange(step_keys)[None, :] // SEL_BLOCK == jnp.arange(step_keys // SEL_BLOCK)[:, None]).astype(BF16)
    ckv_s = _cmp_paged(cache_cmp_kv.reshape(-1, HEAD_DIM), l * n_pool, page_table, w1r, cmp_bias, w2)
    wlen = state_win_kv.shape[2]
    win = state_win_kv[l].reshape(db * wlen * KV_GROUPS, HEAD_DIM)
    o_nsa_s = _nsa_sample(zas, zgs, ckv_s, cache_slc_kv.reshape(-1, HEAD_DIM), l * n_pool, win,
                          page_table, msel_s, e_blocks, gain, ts)
    o_hg_s, hg_s = _hgrn(zhs, hg_lb_logits, hg_out_norm[l], state_hgrn[l], db, ts, l, F32)
    y_sample = trunk_back(x1s, o_nsa_s, o_hg_s).reshape(db, ts, d)
    s_cmp = kv_rows(kvc_s, (db, ts))
    s_slc = kv_rows(kvs_s, (db, ts))
    s_win = jnp.concatenate([state_win_kv[l:l + 1, :, ts:].astype(F32), kv_rows(kvw_s, (db, ts))], axis=2)

    return (y_prompt, y_sample, p_cmp, p_slc, p_win, hg_p[None].astype(state_hgrn.dtype),
            s_cmp, s_slc, s_win, hg_s[None].astype(state_hgrn.dtype))
```

```python
import functools

import jax
import jax.numpy as jnp
from jax import lax
from jax.experimental import pallas as pl
from jax.experimental.pallas import tpu as pltpu

F32 = jnp.float32
BF16 = jnp.bfloat16
NEG_INF = float("-inf")

HEAD_DIM = 128
NSA_KV = 2
NSA_GROUP = 4
NSA_HEADS = NSA_KV * NSA_GROUP
NSA_WIDTH = NSA_HEADS * HEAD_DIM
KV_COLS = 2 * NSA_KV * HEAD_DIM
HG_HEADS = 8
HG_WIDTH = HG_HEADS * HEAD_DIM
CMP_BLOCK = 32
CMP_STRIDE = 16
SEL_BLOCK = 64
SEL_SHIFT = 6
SEL_RATIO = SEL_BLOCK // CMP_STRIDE
N_SEL = 16
SEL_SPAN_W = (1.0, 2.0, 2.0, 2.0, 1.0)
WINDOW = 512
PAGE_SIZE = 128
HG_CHUNK = 64
HG_SUB = 16
ATTN_SCALE = HEAD_DIM ** -0.5
HG_SCALE = HEAD_DIM ** -0.5
EPS = 1e-6
EXP2_SCALE = ATTN_SCALE * 1.4426950408889634
MASK_BIAS = 2.0 ** 40

ZA_COLS = NSA_WIDTH + 3 * KV_COLS
ZH_COLS = 4 * HG_WIDTH
ZG_COLS = 128
PROJ_TN = 512
COL_CMP = NSA_WIDTH
COL_SLC = NSA_WIDTH + KV_COLS
COL_WIN = NSA_WIDTH + 2 * KV_COLS

VMEM_LIMIT = 56 * 1024 * 1024


def _cparams(sem):
    return pltpu.CompilerParams(dimension_semantics=sem, vmem_limit_bytes=VMEM_LIMIT)


def _rms(x, g):
    return x * lax.rsqrt(jnp.mean(x * x, axis=-1, keepdims=True) + EPS) * g


def _dot(a, b):
    return jnp.dot(a, b, preferred_element_type=F32)


def _dot_nt(a, b):
    return lax.dot_general(a, b, (((1,), (1,)), ((), ())), preferred_element_type=F32)


def _dot_tn(a, b):
    return lax.dot_general(a, b, (((0,), (0,)), ((), ())), preferred_element_type=F32)


def _split3(x):
    hi = x.astype(BF16)
    r1 = x - hi.astype(F32)
    mid = r1.astype(BF16)
    lo = (r1 - mid.astype(F32)).astype(BF16)
    return hi, mid, lo


def _ffn_kernel(x_ref, g_ref, wg_ref, wu_ref, wd_ref, pg_ref, *rest, emit_x):
    if emit_x:
        xo_ref, no_ref, h_sc, acc_sc = rest
    else:
        no_ref, h_sc, acc_sc = rest
    j = pl.program_id(1)

    @pl.when(j == 0)
    def _():
        h_sc[...] = _rms(x_ref[...], g_ref[...]).astype(BF16)
        acc_sc[...] = jnp.zeros_like(acc_sc)

    h = h_sc[...]
    a = _dot(h, wg_ref[...])
    u = _dot(h, wu_ref[...])
    act = (a * jax.nn.sigmoid(a) * u).astype(BF16)
    acc_sc[...] += _dot(act, wd_ref[...])

    @pl.when(j == pl.num_programs(1) - 1)
    def _():
        xn = x_ref[...] + 0.5 * acc_sc[...]
        if emit_x:
            xo_ref[...] = xn
        no_ref[...] = _rms(xn, pg_ref[...]).astype(no_ref.dtype)


def _ffn(x, g, wg, wu, wd, post_g, *, emit_x, norm_dtype, tf=512):
    n, d = x.shape
    fp = wg.shape[1]
    tm = min(n, 512)
    assert n % tm == 0 and fp % tf == 0
    row = pl.BlockSpec((tm, d), lambda i, j: (i, 0))
    vec = pl.BlockSpec((1, d), lambda i, j: (0, 0))
    out_shape = [jax.ShapeDtypeStruct((n, d), norm_dtype)]
    out_specs = [row]
    if emit_x:
        out_shape = [jax.ShapeDtypeStruct((n, d), F32)] + out_shape
        out_specs = [row] + out_specs
    return pl.pallas_call(
        functools.partial(_ffn_kernel, emit_x=emit_x),
        grid=(n // tm, fp // tf),
        in_specs=[row, vec,
                  pl.BlockSpec((d, tf), lambda i, j: (0, j)),
                  pl.BlockSpec((d, tf), lambda i, j: (0, j)),
                  pl.BlockSpec((tf, d), lambda i, j: (j, 0)),
                  vec],
        out_specs=out_specs,
        out_shape=out_shape,
        scratch_shapes=[pltpu.VMEM((tm, d), BF16), pltpu.VMEM((tm, d), F32)],
        compiler_params=_cparams(("parallel", "arbitrary")),
        name="ffn_half_step",
    )(x, g.reshape(1, d), wg, wu, wd, post_g.reshape(1, d))


def _inproj_kernel(h_ref, w_ref, za_ref, za16_ref, zh_ref, zg_ref, kvc_ref, kvs_ref, kvw_ref, *, na, nh):
    j = pl.program_id(1)
    tm = h_ref.shape[0]

    def product():
        return _dot(h_ref[...], w_ref[...])

    @pl.when(j < COL_CMP // PROJ_TN)
    def _():
        r = product()
        za_ref[...] = r
        za16_ref[...] = r.astype(BF16)

    for col, kv_ref in ((COL_CMP, kvc_ref), (COL_SLC, kvs_ref), (COL_WIN, kvw_ref)):
        @pl.when(j == col // PROJ_TN)
        def _(kv_ref=kv_ref):
            r = product()
            za_ref[...] = r
            za16_ref[...] = r.astype(BF16)
            for ck in range(KV_GROUPS):
                kv_ref[pl.ds(ck, tm, stride=KV_GROUPS), :] = r[:, ck * HEAD_DIM:(ck + 1) * HEAD_DIM]

    @pl.when((j >= na) & (j < na + nh))
    def _():
        zh_ref[...] = product()

    @pl.when(j == na + nh)
    def _():
        zg_ref[...] = product()[:, :ZG_COLS]


def _inproj(h, w):
    n, d = h.shape
    tn = PROJ_TN
    na, nh = ZA_COLS // tn, ZH_COLS // tn
    tm = min(n, 1024)
    assert n % tm == 0 and w.shape[1] == (na + nh + 1) * tn and tn == KV_COLS
    kv_spec = pl.BlockSpec((tm * KV_GROUPS, HEAD_DIM), lambda i, j: (i, 0))
    kv_shape = jax.ShapeDtypeStruct((n * KV_GROUPS, HEAD_DIM), F32)
    return pl.pallas_call(
        functools.partial(_inproj_kernel, na=na, nh=nh),
        grid=(n // tm, na + nh + 1),
        in_specs=[pl.BlockSpec((tm, d), lambda i, j: (i, 0)),
                  pl.BlockSpec((d, tn), lambda i, j: (0, j))],
        out_specs=[pl.BlockSpec((tm, tn), lambda i, j: (i, jnp.minimum(j, na - 1))),
                   pl.BlockSpec((tm, tn), lambda i, j: (i, jnp.minimum(j, na - 1))),
                   pl.BlockSpec((tm, tn), lambda i, j: (i, jnp.clip(j - na, 0, nh - 1))),
                   pl.BlockSpec((tm, ZG_COLS), lambda i, j: (i, 0)),
                   kv_spec, kv_spec, kv_spec],
        out_shape=[jax.ShapeDtypeStruct((n, ZA_COLS), F32),
                   jax.ShapeDtypeStruct((n, ZA_COLS), BF16),
                   jax.ShapeDtypeStruct((n, ZH_COLS), F32),
                   jax.ShapeDtypeStruct((n, ZG_COLS), F32),
                   kv_shape, kv_shape, kv_shape],
        compiler_params=_cparams(("parallel", "arbitrary")),
        name="in_projection",
    )(h, w)


def _outproj_kernel(x_ref, a_ref, b_ref, w_ref, o_ref):
    half = a_ref.shape[1]
    o_ref[...] = (x_ref[...]
                  + _dot(a_ref[...].astype(BF16), w_ref[0:half, :])
                  + _dot(b_ref[...].astype(BF16), w_ref[half:, :]))


def _outproj(x, a, b, w):
    n, d = x.shape
    half = a.shape[1]
    tm = min(n, 512)
    assert n % tm == 0
    return pl.pallas_call(
        _outproj_kernel,
        grid=(n // tm,),
        in_specs=[pl.BlockSpec((tm, d), lambda i: (i, 0)),
                  pl.BlockSpec((tm, half), lambda i: (i, 0)),
                  pl.BlockSpec((tm, half), lambda i: (i, 0)),
                  pl.BlockSpec((2 * half, d), lambda i: (0, 0))],
        out_specs=pl.BlockSpec((tm, d), lambda i: (i, 0)),
        out_shape=jax.ShapeDtypeStruct((n, d), F32),
        compiler_params=_cparams(("parallel",)),
        name="out_projection",
    )(x, a, b, w)


def _cmp_bias_kernel(pe_ref, w1_ref, b1_ref, o_ref):
    for c in range(2):
        pe8 = jnp.broadcast_to(pe_ref[c], (8, pe_ref.shape[2])).astype(BF16)
        o_ref[c] = _dot(pe8, w1_ref[c])[0:1, :] + b1_ref[c]


def _cmp_bias(pe, w1, b1):
    flat = CMP_BLOCK * HEAD_DIM
    return pl.pallas_call(
        _cmp_bias_kernel,
        out_shape=jax.ShapeDtypeStruct((2, 1, HEAD_DIM), F32),
        name="cmp_bias",
    )(pe.reshape(2, 1, flat), w1.reshape(2, flat, HEAD_DIM).astype(BF16), b1.reshape(2, 1, HEAD_DIM))


def _cmp_accumulate(get_rows, w1r_ref, hp_sc, seg0, nss):
    for c in range(2):
        acc = None
        for r in range(CMP_STRIDE):
            xr = jnp.concatenate([get_rows(r, 2 * c + k) for k in range(NSA_KV)], axis=0).astype(BF16)
            d = _dot(xr, w1r_ref[c, r])
            acc = d if acc is None else acc + d
        for k in range(NSA_KV):
            hp_sc[2 * c + k, pl.ds(seg0, nss), :] = acc[k * nss:(k + 1) * nss]


def _cmp_finish(hp_sc, bias_ref, w2_ref, o_ref):
    nseg = hp_sc.shape[1]
    last = lax.broadcasted_iota(jnp.int32, (nseg, 1), 0) == nseg - 1
    for ck in range(2 * NSA_KV):
        c = ck // NSA_KV
        first = hp_sc[ck, :, 0:HEAD_DIM]
        second = pltpu.roll(hp_sc[ck, :, HEAD_DIM:2 * HEAD_DIM], nseg - 1, 0)
        pre = first + second + bias_ref[c]
        out = _dot((pre * jax.nn.sigmoid(pre)).astype(BF16), w2_ref[c])
        o_ref[0, ck] = jnp.where(last, 0.0, out).astype(o_ref.dtype)


def _cmp_prompt_kernel(x0_ref, x1_ref, x2_ref, x3_ref, w1r_ref, bias_ref, w2_ref, o_ref, hp_sc, *, nss):
    j = pl.program_id(1)
    x_refs = (x0_ref, x1_ref, x2_ref, x3_ref)

    def get_rows(r, ck):
        return x_refs[ck][pl.ds(r, nss, stride=CMP_STRIDE), :]

    _cmp_accumulate(get_rows, w1r_ref, hp_sc, pl.multiple_of(j * nss, nss), nss)

    @pl.when(j == pl.num_programs(1) - 1)
    def _():
        _cmp_finish(hp_sc, bias_ref, w2_ref, o_ref)


def _cmp_weight_specs(grid_rank):
    zeros = lambda n: (lambda *a: (0,) * n)
    return [pl.BlockSpec((2, CMP_STRIDE, HEAD_DIM, 2 * HEAD_DIM), zeros(4)),
            pl.BlockSpec((2, 1, HEAD_DIM), zeros(3)),
            pl.BlockSpec((2, HEAD_DIM, HEAD_DIM), zeros(3))]


def _cmp_prompt(za, b, t, w1r, bias, w2):
    rc = min(t, 2048)
    nss = rc // CMP_STRIDE
    nseg = t // CMP_STRIDE
    steps = t // rc
    assert t % rc == 0
    return pl.pallas_call(
        functools.partial(_cmp_prompt_kernel, nss=nss),
        grid=(b, steps),
        in_specs=[pl.BlockSpec((rc, HEAD_DIM), functools.partial(lambda i, j, ck: (i * steps + j, COL_CMP // HEAD_DIM + ck), ck=ck))
                  for ck in range(2 * NSA_KV)] + _cmp_weight_specs(2),
        out_specs=pl.BlockSpec((1, 2 * NSA_KV, nseg, HEAD_DIM), lambda i, j: (i, 0, 0, 0)),
        out_shape=jax.ShapeDtypeStruct((b, 2 * NSA_KV, nseg, HEAD_DIM), BF16),
        scratch_shapes=[pltpu.VMEM((2 * NSA_KV, nseg, 2 * HEAD_DIM), F32)],
        compiler_params=_cparams(("parallel", "arbitrary")),
        name="cmp_mlp_prompt",
    )(za, za, za, za, w1r, bias, w2)


CMP_PAGES = 32


KV_GROUPS = 2 * NSA_KV
PAGE_ROWS = PAGE_SIZE * KV_GROUPS


def _cmp_paged_kernel(pt_ref, *refs):
    pages = refs[:CMP_PAGES]
    perm_ref, w1p_ref, bias_ref, w2_ref, o_ref, hp_sc, x_sc = refs[CMP_PAGES:]
    j = pl.program_id(1)
    spp = PAGE_SIZE // CMP_STRIDE
    nss = CMP_PAGES * spp
    seg0 = pl.multiple_of(j * nss, nss)
    for c in range(2):
        for g, pg in enumerate(pages):
            xc = jnp.concatenate([pg[pl.ds(NSA_KV * c + k, PAGE_SIZE, stride=KV_GROUPS), :] for k in range(NSA_KV)],
                                 axis=1).astype(BF16)
            xp = _dot(perm_ref[...], xc)
            for r in range(CMP_STRIDE):
                for k in range(NSA_KV):
                    x_sc[c, r, k, g * spp:(g + 1) * spp, :] = xp[r * spp:(r + 1) * spp, k * HEAD_DIM:(k + 1) * HEAD_DIM]
    for c in range(2):
        for k in range(NSA_KV):
            lhs = jnp.concatenate([x_sc[c, r, k] for r in range(CMP_STRIDE)], axis=1).astype(BF16)
            hp_sc[2 * c + k, pl.ds(seg0, nss), :] = _dot(lhs, w1p_ref[c])

    @pl.when(j == pl.num_programs(1) - 1)
    def _():
        _cmp_finish(hp_sc, bias_ref, w2_ref, o_ref)


def _page_specs(n, page0):
    def spec(g):
        return pl.BlockSpec((PAGE_ROWS, HEAD_DIM), lambda i, j, pt: (page0 + pt[i, j * n + g], 0))
    return [spec(g) for g in range(n)]


def _cmp_paged(cache, page0, page_table, w1r, bias, w2):
    db, n_pages = page_table.shape
    assert n_pages % CMP_PAGES == 0 and PAGE_SIZE == HEAD_DIM
    nseg = n_pages * PAGE_SIZE // CMP_STRIDE
    spp = PAGE_SIZE // CMP_STRIDE
    nss = CMP_PAGES * spp
    i = jnp.arange(PAGE_SIZE)
    perm = (i[None, :] == ((i % spp) * CMP_STRIDE + i // spp)[:, None]).astype(BF16)
    w1p = w1r.reshape(2, CMP_STRIDE * HEAD_DIM, 2 * HEAD_DIM)
    zeros = lambda n: (lambda i, j, pt: (0,) * n)
    grid_spec = pltpu.PrefetchScalarGridSpec(
        num_scalar_prefetch=1,
        grid=(db, n_pages // CMP_PAGES),
        in_specs=_page_specs(CMP_PAGES, page0) + [
            pl.BlockSpec((PAGE_SIZE, PAGE_SIZE), zeros(2)),
            pl.BlockSpec((2, CMP_STRIDE * HEAD_DIM, 2 * HEAD_DIM), zeros(3)),
            pl.BlockSpec((2, 1, HEAD_DIM), zeros(3)),
            pl.BlockSpec((2, HEAD_DIM, HEAD_DIM), zeros(3))],
        out_specs=pl.BlockSpec((1, 2 * NSA_KV, nseg, HEAD_DIM), lambda i, j, pt: (i, 0, 0, 0)),
        scratch_shapes=[pltpu.VMEM((2 * NSA_KV, nseg, 2 * HEAD_DIM), F32),
                        pltpu.VMEM((2, CMP_STRIDE, NSA_KV, nss, HEAD_DIM), F32)])
    return pl.pallas_call(
        _cmp_paged_kernel,
        grid_spec=grid_spec,
        out_shape=jax.ShapeDtypeStruct((db, 2 * NSA_KV, nseg, HEAD_DIM), BF16),
        compiler_params=_cparams(("parallel", "arbitrary")),
        name="cmp_mlp_paged",
    )(page_table, *([cache] * CMP_PAGES), perm, w1p, bias, w2)


def _softmax0(s):
    m = jnp.max(s, axis=0, keepdims=True)
    m = jnp.where(m > NEG_INF, m, 0.0)
    e = jnp.exp2((s - m) * EXP2_SCALE)
    l = jnp.sum(e, axis=0, keepdims=True)
    return e * (1.0 / jnp.maximum(l, 1e-30))


def _sel_importance_t(msel, u):
    hi, mid, lo = _split3(u)
    return _dot(msel, hi) + _dot(msel, mid) + _dot(msel, lo)


def _topk_t(pslc, tpos, nsel):
    nsp = pslc.shape[0]
    jio = lax.broadcasted_iota(jnp.int32, pslc.shape, 0)
    jt = lax.shift_right_logical(tpos, SEL_SHIFT)
    forced = (jio == 0) | (jio == jt) | (jio == jt - 1)
    score0 = jnp.where(forced, jnp.inf, jnp.where(jio <= jt, pslc, NEG_INF))
    score = score0
    for _ in range(nsel):
        m = jnp.max(score, axis=0, keepdims=True)
        idx = jnp.min(jnp.where(score == m, jio, nsp), axis=0, keepdims=True)
        score = jnp.where(jio == idx, NEG_INF, score)
    return jnp.where(score != score0, 1.0, 0.0)


def _nsa_prompt_kernel(q_ref, ks_ref, vs_ref, kw_ref, vw_ref, eb_ref, ck_ref, cv_ref, zg_ref, msel_ref, gain_ref,
                       o_ref, gate_sc, s_sc, m_sc, l_sc, acc_sc, *, tq, tk, nsel, chains):
    k = pl.program_id(1)
    i = pl.program_id(2)
    t0 = i * tq
    r = NSA_GROUP * tq
    q = q_ref[...]
    qr = jnp.concatenate([q[:, g * HEAD_DIM:(g + 1) * HEAD_DIM] for g in range(NSA_GROUP)], axis=0)
    tpos1 = t0 + lax.broadcasted_iota(jnp.int32, (1, tq), 1)
    tpos = jnp.concatenate([tpos1] * NSA_GROUP, axis=1)

    ncp = ck_ref.shape[2]
    wlen = WINDOW + tq
    ws = pl.multiple_of(jnp.maximum(t0 - WINDOW, 0), tq)
    s_cmp = _dot_nt(ck_ref[0, 0], qr)
    s_win = _dot_nt(kw_ref[pl.ds(ws, wlen), :], qr)

    cend = lax.broadcasted_iota(jnp.int32, (ncp, 1), 0) * CMP_STRIDE + (CMP_BLOCK - 1)
    p = _softmax0(jnp.where(cend <= tpos, s_cmp, NEG_INF))
    imp = p[:, 0:tq]
    for g in range(1, NSA_GROUP):
        imp = imp + p[:, g * tq:(g + 1) * tq]
    pslc = _sel_importance_t(msel_ref[...], imp)
    o_cmp = _dot_tn(cv_ref[0, 0], p.astype(BF16))

    d = tpos - (ws + lax.broadcasted_iota(jnp.int32, (wlen, 1), 0))
    in_window = lax.bitcast_convert_type(d, jnp.uint32) <= WINDOW
    sw = jnp.where(in_window, s_win, NEG_INF)
    ew = jnp.exp2((sw - jnp.max(sw, axis=0, keepdims=True)) * EXP2_SCALE)
    lw = jnp.sum(ew, axis=0, keepdims=True)
    o_win = _dot_tn(vw_ref[pl.ds(ws, wlen), :], ew.astype(BF16)) * (1.0 / jnp.maximum(lw, 1e-30))

    sel = _topk_t(pslc, tpos1, nsel)

    selm1 = (sel - 1.0).T.astype(BF16)
    q_aug = jnp.concatenate([qr, jnp.concatenate([selm1] * NSA_GROUP, axis=0)], axis=1)
    ones8 = jnp.ones((8, tk), BF16)
    rc = r // chains

    def k_tile(kt):
        off = pl.multiple_of(kt * tk, tk)
        return jnp.concatenate([ks_ref[pl.ds(off, tk), :], eb_ref[pl.ds(off, tk), :]], axis=1)

    def put_scores(c, k_aug):
        s_sc[c] = _dot_nt(k_aug, q_aug[c * rc:(c + 1) * rc])

    def softmax_tile(c, kt, m, causal):
        s = s_sc[c]
        if causal:
            visible = kt * tk + lax.broadcasted_iota(jnp.int32, (tk, 1), 0) <= tpos[:, c * rc:(c + 1) * rc]
            s = jnp.where(visible, s, NEG_INF)
        m_new = jnp.maximum(m, jnp.max(s, axis=0, keepdims=True))
        s = s_sc[c]
        if causal:
            s = jnp.where(visible, s, NEG_INF)
        return m_new, jnp.exp2((s - m_new) * EXP2_SCALE).astype(BF16)

    def accumulate(c, m, m_new, e, v):
        alpha = jnp.exp2((m - m_new) * EXP2_SCALE)
        l_sc[c] = alpha * l_sc[c] + _dot(ones8, e)[0:1, :]
        acc_sc[c] = alpha * acc_sc[c] + _dot_tn(v, e)
        m_sc[c] = m_new

    def step(kt):
        v = vs_ref[pl.ds(pl.multiple_of(kt * tk, tk), tk), :]
        k_next = k_tile(kt + 1)
        pending = None
        for c in range(chains):
            m = m_sc[c]
            m_new, e = softmax_tile(c, kt, m, False)
            if pending is not None:
                accumulate(*pending)
            put_scores(c, k_next)
            pending = (c, m, m_new, e, v)
        accumulate(*pending)

    unroll = 4

    def step_group(gi, carry):
        for u in range(unroll):
            step(unroll * gi + u)
        return carry

    n_full = t0 // tk
    k0 = k_tile(0)
    for c in range(chains):
        put_scores(c, k0)
        m_sc[c] = jnp.full((1, rc), NEG_INF, F32)
        l_sc[c] = jnp.zeros((1, rc), F32)
        acc_sc[c] = jnp.zeros((HEAD_DIM, rc), F32)
    lax.fori_loop(0, n_full // unroll, step_group, 0)
    done = (n_full // unroll) * unroll
    width = unroll // 2
    while width >= 1:
        @pl.when((n_full & width) != 0)
        def _(width=width, done=done):
            for u in range(width):
                step(done + u)
        done = done + (n_full & width)
        width //= 2

    v = vs_ref[pl.ds(pl.multiple_of(n_full * tk, tk), tk), :]
    for c in range(chains):
        m = m_sc[c]
        m_new, e = softmax_tile(c, n_full, m, True)
        accumulate(c, m, m_new, e, v)
    o_slc = jnp.concatenate([acc_sc[c] * (1.0 / jnp.maximum(l_sc[c], 1e-30)) for c in range(chains)], axis=1)

    gate_sc[...] = jax.nn.sigmoid(zg_ref[...].T[0:gate_sc.shape[0], :])

    def gate_row(branch):
        return jnp.concatenate(
            [gate_sc[pl.ds(branch * NSA_HEADS + k * NSA_GROUP + g, 1), :] for g in range(NSA_GROUP)], axis=1)

    o = gate_row(0) * o_cmp + gate_row(1) * o_slc + gate_row(2) * o_win
    o = o * lax.rsqrt(jnp.mean(o * o, axis=0, keepdims=True) + EPS)
    for g in range(NSA_GROUP):
        og = o[:, g * tq:(g + 1) * tq] * gain_ref[k * NSA_GROUP + g]
        o_ref[:, g * HEAD_DIM:(g + 1) * HEAD_DIM] = og.T.astype(o_ref.dtype)


NSA_TQ = 256
NSA_CHAINS = 2


def _nsa_prompt(za16, ckv, zg, msel, gain, b, t):
    tq = NSA_TQ
    tk = min(t, 512)
    gain_b = jnp.broadcast_to(gain[:, :, None], (NSA_HEADS, HEAD_DIM, tq))
    nq = t // tq
    ncp = ckv.shape[2]
    nsp = msel.shape[0]
    qw = NSA_GROUP * HEAD_DIM
    assert t % tk == 0 and tk % tq == 0 and t >= WINDOW + tq and nsp == 128
    col = lambda base, kk: (base // HEAD_DIM) + kk
    block_bias = jnp.where(jnp.arange(t)[:, None] // SEL_BLOCK == jnp.arange(nsp)[None, :], MASK_BIAS, 0.0).astype(BF16)

    def seq_spec(base):
        return pl.BlockSpec((t, HEAD_DIM), lambda bi, k, i: (bi, col(base, k)))

    return pl.pallas_call(
        functools.partial(_nsa_prompt_kernel, tq=tq, tk=tk, nsel=min(N_SEL, t // SEL_BLOCK), chains=NSA_CHAINS),
        grid=(b, NSA_KV, nq),
        in_specs=[pl.BlockSpec((tq, qw), lambda bi, k, i: (bi * nq + i, k)),
                  seq_spec(COL_SLC), seq_spec(COL_SLC + NSA_KV * HEAD_DIM),
                  seq_spec(COL_WIN), seq_spec(COL_WIN + NSA_KV * HEAD_DIM),
                  pl.BlockSpec((t, nsp), lambda bi, k, i: (0, 0)),
                  pl.BlockSpec((1, 1, ncp, HEAD_DIM), lambda bi, k, i: (bi, k, 0, 0)),
                  pl.BlockSpec((1, 1, ncp, HEAD_DIM), lambda bi, k, i: (bi, NSA_KV + k, 0, 0)),
                  pl.BlockSpec((tq, ZG_COLS), lambda bi, k, i: (bi * nq + i, 0)),
                  pl.BlockSpec((nsp, ncp), lambda bi, k, i: (0, 0)),
                  pl.BlockSpec((NSA_HEADS, HEAD_DIM, tq), lambda bi, k, i: (0, 0, 0))],
        out_specs=pl.BlockSpec((tq, qw), lambda bi, k, i: (bi * nq + i, k)),
        out_shape=jax.ShapeDtypeStruct((b * t, NSA_WIDTH), BF16),
        scratch_shapes=[pltpu.VMEM((32, tq), F32),
                        pltpu.VMEM((NSA_CHAINS, tk, NSA_GROUP * tq // NSA_CHAINS), F32),
                        pltpu.VMEM((NSA_CHAINS, 1, NSA_GROUP * tq // NSA_CHAINS), F32),
                        pltpu.VMEM((NSA_CHAINS, 1, NSA_GROUP * tq // NSA_CHAINS), F32),
                        pltpu.VMEM((NSA_CHAINS, HEAD_DIM, NSA_GROUP * tq // NSA_CHAINS), F32)],
        compiler_params=_cparams(("parallel", "parallel", "arbitrary")),
        name="nsa_prompt",
    )(za16, za16, za16, za16, za16, block_bias, ckv, ckv, zg, msel, gain_b)


SLC_PAGES = 32


def _nsa_sample_kernel(pt_ref, *refs, ts, past_len):
    pages = refs[:SLC_PAGES]
    (z_ref, zg_ref, ckv_ref, win_ref, msel_ref, e_ref, gain_ref,
     o_ref, sel_sc, m_sc, l_sc, acc_sc, ocmp_sc) = refs[SLC_PAGES:]
    j = pl.program_id(1)
    rows = NSA_GROUP * ts
    lanes = 128
    step_keys = SLC_PAGES * PAGE_SIZE
    step_blocks = step_keys // SEL_BLOCK
    z = z_ref[...]

    def q_rows(k):
        return jnp.concatenate([z[:, (k * NSA_GROUP + g) * HEAD_DIM:(k * NSA_GROUP + g + 1) * HEAD_DIM]
                                for g in range(NSA_GROUP)], axis=0).astype(BF16)

    @pl.when(j == 0)
    def _():
        lane = lax.broadcasted_iota(jnp.int32, (1, lanes), 1)
        tpos = past_len + jnp.bitwise_and(lane, ts - 1)
        s = None
        for k in range(NSA_KV):
            pads = (k * rows, lanes - (k + 1) * rows)
            parts = [jnp.zeros((pads[0], HEAD_DIM), BF16), q_rows(k), jnp.zeros((pads[1], HEAD_DIM), BF16)]
            qp = jnp.concatenate([x for x in parts if x.shape[0] > 0], axis=0)
            sk = _dot_nt(ckv_ref[0, k], qp)
            s = sk if s is None else s + sk
        ncp = s.shape[0]
        cend = lax.broadcasted_iota(jnp.int32, (ncp, 1), 0) * CMP_STRIDE + (CMP_BLOCK - 1)
        p = _softmax0(jnp.where(cend <= tpos, s, NEG_INF))
        p16 = p.astype(BF16)
        for k in range(NSA_KV):
            ocmp_sc[k] = _dot_tn(ckv_ref[0, NSA_KV + k], p16).T[k * rows:(k + 1) * rows, :]
            m_sc[k] = jnp.full((rows, 1), NEG_INF, F32)
            l_sc[k] = jnp.zeros((rows, 1), F32)
            acc_sc[k] = jnp.zeros((rows, HEAD_DIM), F32)
        u = _sel_importance_t(msel_ref[...], p)
        pslc = u
        for g in range(1, NSA_GROUP):
            pslc = pslc + pltpu.roll(u, lanes - g * ts, 1)
        sel_sc[...] = _topk_t(pslc, tpos, N_SEL)

    def online(k, s, v):
        m_old = m_sc[k]
        m_new = jnp.maximum(m_old, jnp.max(s, axis=1, keepdims=True))
        alpha = jnp.exp2((m_old - m_new) * EXP2_SCALE)
        e = jnp.exp2((s - m_new) * EXP2_SCALE)
        l_sc[k] = alpha * l_sc[k] + jnp.sum(e, axis=1, keepdims=True)
        acc_sc[k] = alpha * acc_sc[k] + _dot(e.astype(BF16), v)
        m_sc[k] = m_new

    def kv_rows(ref, ck, n):
        return ref[pl.ds(ck, n, stride=KV_GROUPS), :].astype(BF16)

    selj = sel_sc[pl.ds(pl.multiple_of(j * step_blocks, step_blocks), step_blocks), :]
    mask_all = _dot_tn(selj.astype(BF16), e_ref[...])
    staged = []
    for k in range(NSA_KV):
        kt = jnp.concatenate([kv_rows(pg, k, PAGE_SIZE) for pg in pages], axis=0)
        mask = jnp.concatenate([mask_all[k * rows:k * rows + ts]] * NSA_GROUP, axis=0) > 0.5
        staged.append(jnp.where(mask, _dot_nt(q_rows(k), kt), NEG_INF))
    for k in range(NSA_KV):
        vt = jnp.concatenate([kv_rows(pg, NSA_KV + k, PAGE_SIZE) for pg in pages], axis=0)
        online(k, staged[k], vt)

    @pl.when(j == pl.num_programs(1) - 1)
    def _():
        row_t = jnp.bitwise_and(lax.broadcasted_iota(jnp.int32, (rows, 1), 0), ts - 1)
        new_i = lax.broadcasted_iota(jnp.int32, (1, 2 * ts), 1)
        new_ok = (new_i < ts) & (new_i <= row_t)
        pad = jnp.zeros((ts, HEAD_DIM), F32)

        def new_rows(col):
            return jnp.concatenate([z[:, col:col + HEAD_DIM], pad], axis=0).astype(BF16)

        wlen = win_ref.shape[0] // KV_GROUPS
        wpos = past_len - wlen + lax.broadcasted_iota(jnp.int32, (1, wlen), 1)
        dw = (past_len + row_t) - wpos
        win_ok = (dw >= 0) & (dw <= WINDOW) & (wpos >= 0)
        zg = zg_ref[...]
        raw = []
        for k in range(NSA_KV):
            q = q_rows(k)
            raw.append((_dot_nt(q, new_rows(COL_SLC + k * HEAD_DIM)),
                        _dot_nt(q, kv_rows(win_ref, k, wlen)),
                        _dot_nt(q, new_rows(COL_WIN + k * HEAD_DIM))))
        for k in range(NSA_KV):
            s_new, s_buf, s_wnew = raw[k]
            online(k, jnp.where(new_ok, s_new, NEG_INF), new_rows(COL_SLC + (NSA_KV + k) * HEAD_DIM))
            o_slc = acc_sc[k] * (1.0 / jnp.maximum(l_sc[k], 1e-30))
            vw = kv_rows(win_ref, NSA_KV + k, wlen)
            s1 = jnp.where(win_ok, s_buf, NEG_INF)
            s2 = jnp.where(new_ok, s_wnew, NEG_INF)
            mw = jnp.maximum(jnp.max(s1, axis=1, keepdims=True), jnp.max(s2, axis=1, keepdims=True))
            e1 = jnp.exp2((s1 - mw) * EXP2_SCALE)
            e2 = jnp.exp2((s2 - mw) * EXP2_SCALE)
            lw = jnp.sum(e1, axis=1, keepdims=True) + jnp.sum(e2, axis=1, keepdims=True)
            o_win = (_dot(e1.astype(BF16), vw)
                     + _dot(e2.astype(BF16), new_rows(COL_WIN + (NSA_KV + k) * HEAD_DIM))) * (1.0 / jnp.maximum(lw, 1e-30))

            def gate(branch):
                c0 = branch * NSA_HEADS + k * NSA_GROUP
                return jnp.concatenate([jax.nn.sigmoid(zg[:, c0 + g:c0 + g + 1]) for g in range(NSA_GROUP)], axis=0)

            o = gate(0) * ocmp_sc[k] + gate(1) * o_slc + gate(2) * o_win
            o = o * lax.rsqrt(jnp.mean(o * o, axis=1, keepdims=True) + EPS)
            for g in range(NSA_GROUP):
                h = k * NSA_GROUP + g
                o_ref[:, h * HEAD_DIM:(h + 1) * HEAD_DIM] = o[g * ts:(g + 1) * ts, :] * gain_ref[h:h + 1, :]


def _nsa_sample(za, zg, ckv, cache_slc, page0, win, page_table, msel, e_blocks, gain, ts):
    db, n_pages = page_table.shape
    past_len = n_pages * PAGE_SIZE
    ncp = ckv.shape[2]
    nsp = msel.shape[0]
    wrows = win.shape[0] // db
    rows = NSA_GROUP * ts
    step_keys = SLC_PAGES * PAGE_SIZE
    assert n_pages % SLC_PAGES == 0 and ts & (ts - 1) == 0 and ts % 8 == 0 and rows <= 128
    assert past_len % SEL_BLOCK == 0 and ts <= SEL_BLOCK and ts < CMP_STRIDE
    zeros = lambda n: (lambda i, j, pt: (0,) * n)
    grid_spec = pltpu.PrefetchScalarGridSpec(
        num_scalar_prefetch=1,
        grid=(db, n_pages // SLC_PAGES),
        in_specs=_page_specs(SLC_PAGES, page0) + [
            pl.BlockSpec((ts, ZA_COLS), lambda i, j, pt: (i, 0)),
            pl.BlockSpec((ts, ZG_COLS), lambda i, j, pt: (i, 0)),
            pl.BlockSpec((1, 2 * NSA_KV, ncp, HEAD_DIM), lambda i, j, pt: (i, 0, 0, 0)),
            pl.BlockSpec((wrows, HEAD_DIM), lambda i, j, pt: (i, 0)),
            pl.BlockSpec((nsp, ncp), zeros(2)),
            pl.BlockSpec((step_keys // SEL_BLOCK, step_keys), zeros(2)),
            pl.BlockSpec((NSA_HEADS, HEAD_DIM), zeros(2))],
        out_specs=pl.BlockSpec((ts, NSA_WIDTH), lambda i, j, pt: (i, 0)),
        scratch_shapes=[pltpu.VMEM((nsp, 128), F32),
                        pltpu.VMEM((NSA_KV, rows, 1), F32),
                        pltpu.VMEM((NSA_KV, rows, 1), F32),
                        pltpu.VMEM((NSA_KV, rows, HEAD_DIM), F32),
                        pltpu.VMEM((NSA_KV, rows, HEAD_DIM), F32)])
    return pl.pallas_call(
        functools.partial(_nsa_sample_kernel, ts=ts, past_len=past_len),
        grid_spec=grid_spec,
        out_shape=jax.ShapeDtypeStruct((db * ts, NSA_WIDTH), F32),
        compiler_params=_cparams(("parallel", "arbitrary")),
        name="nsa_sample",
    )(page_table, *([cache_slc] * SLC_PAGES), za, zg, ckv, win, msel, e_blocks, gain)


def _cumsum0(x):
    n = x.shape[0]
    row = lax.broadcasted_iota(jnp.int32, (n, 1), 0)
    sh = 1
    while sh < n:
        x = x + jnp.where(row >= sh, pltpu.roll(x, sh, 0), 0.0)
        sh *= 2
    return x


HG_PAR = 4


def _hgrn_heads_chunk(heads, *, c, sb):
    row8 = lax.broadcasted_iota(jnp.int32, (8, 1), 0)
    pre = []
    for hq, hf, v, gt, lb, gain, st in heads:
        fg = lb + (1.0 - lb) * jax.nn.sigmoid(hf)
        kk = 1.0 - fg
        bc = _cumsum0(jnp.log2(fg))
        qh = hq * jax.nn.sigmoid(hq) * HG_SCALE
        pre.append((kk, bc, qh, v.astype(BF16)))
    o_inter = [_dot_nt((qh * jnp.exp2(bc)).astype(BF16), head[6].astype(BF16))
               for (kk, bc, qh, v16), head in zip(pre, heads)]
    pieces = [[] for _ in heads]
    for si in range(c // sb):
        lo = si * sb
        scores = []
        if si > 0:
            for kk, bc, qh, v16 in pre:
                bcr = bc[lo - 1:lo]
                qs = (qh[lo:lo + sb] * jnp.exp2(bc[lo:lo + sb] - bcr)).astype(BF16)
                kp = (kk[0:lo] * jnp.exp2(bcr - bc[0:lo])).astype(BF16)
                scores.append(_dot_nt(qs, kp).astype(BF16))
        for hi, ((kk, bc, qh, v16), head) in enumerate(zip(pre, heads)):
            v = head[2]
            od = [jnp.zeros((8, HEAD_DIM), F32) for _ in range(sb // 8)]
            for s in range(sb):
                ks, vs_, bs = kk[lo + s:lo + s + 1], v[lo + s:lo + s + 1], bc[lo + s:lo + s + 1]
                for p in range(s // 8, sb // 8):
                    r0 = lo + 8 * p
                    diff = bc[r0:r0 + 8] - bs
                    if p == s // 8:
                        diff = jnp.where(row8 >= s % 8, diff, NEG_INF)
                    a = jnp.sum(qh[r0:r0 + 8] * ks * jnp.exp2(diff), axis=1, keepdims=True)
                    od[p] = od[p] + a * vs_
            od = od[0] if len(od) == 1 else jnp.concatenate(od, axis=0)
            if si > 0:
                od = od + _dot(scores[hi], v16[0:lo])
            pieces[hi].append(od)
    out = []
    for (kk, bc, qh, v16), head, oi, pc in zip(pre, heads, o_inter, pieces):
        gt, gain, st = head[3], head[5], head[6]
        o = oi + (pc[0] if len(pc) == 1 else jnp.concatenate(pc, axis=0))
        bl = bc[c - 1:c]
        st_new = st * jnp.exp2(bl) + _dot_tn(v16, (kk * jnp.exp2(bl - bc)).astype(BF16))
        out.append((_rms(o, gain) * (gt * jax.nn.sigmoid(gt)), st_new))
    return out


def _hgrn_kernel(q_ref, f_ref, i_ref, g_ref, lbl_ref, on_ref, s0_ref, o_ref, so_ref, st_sc, *, c, sb, layer):
    j = pl.program_id(2)
    nh = st_sc.shape[0]

    @pl.when(j == 0)
    def _():
        for h in range(nh):
            st_sc[h] = s0_ref[0, h].T

    lbl = lbl_ref[...]
    e = jnp.exp(lbl - jnp.max(lbl, axis=0, keepdims=True))
    lb = jnp.sum(e[0:layer + 1], axis=0, keepdims=True) / jnp.sum(e, axis=0, keepdims=True)
    gain = on_ref[...]

    def chunk(ci, carry):
        rows = pl.ds(pl.multiple_of(ci * c, c), c)
        col = lambda h: slice(h * HEAD_DIM, (h + 1) * HEAD_DIM)
        heads = [(q_ref[rows, col(h)], f_ref[rows, col(h)], i_ref[rows, col(h)], g_ref[rows, col(h)],
                  lb[:, col(h)], gain[:, col(h)], st_sc[h]) for h in range(nh)]
        for h, (y, st_new) in enumerate(_hgrn_heads_chunk(heads, c=c, sb=sb)):
            st_sc[h] = st_new
            o_ref[rows, col(h)] = y.astype(o_ref.dtype)
        return carry

    lax.fori_loop(0, q_ref.shape[0] // c, chunk, 0)

    @pl.when(j == pl.num_programs(2) - 1)
    def _():
        for h in range(nh):
            so_ref[0, h] = st_sc[h].T


def _hgrn(zh, lb_logits, out_norm, s0, b, t, layer, out_dtype):
    c = HG_CHUNK if t % HG_CHUNK == 0 else t
    sb = min(HG_SUB, c)
    tc = min(t, 512)
    nt = t // tc
    hw = HG_PAR * HEAD_DIM
    ng = HG_HEADS // HG_PAR
    assert t % tc == 0 and tc % c == 0 and c % sb == 0 and sb % 8 == 0 and HG_HEADS % HG_PAR == 0

    def piece(p):
        return pl.BlockSpec((tc, hw), lambda bi, h, j: (bi * nt + j, p * ng + h))

    st_spec = pl.BlockSpec((1, HG_PAR, HEAD_DIM, HEAD_DIM), lambda bi, h, j: (bi, h, 0, 0))
    nl = lb_logits.shape[0]
    return pl.pallas_call(
        functools.partial(_hgrn_kernel, c=c, sb=sb, layer=layer),
        grid=(b, ng, nt),
        in_specs=[piece(0), piece(1), piece(2), piece(3),
                  pl.BlockSpec((nl, hw), lambda bi, h, j: (0, h)),
                  pl.BlockSpec((1, hw), lambda bi, h, j: (0, h)),
                  st_spec],
        out_specs=[pl.BlockSpec((tc, hw), lambda bi, h, j: (bi * nt + j, h)), st_spec],
        out_shape=[jax.ShapeDtypeStruct((b * t, HG_WIDTH), out_dtype),
                   jax.ShapeDtypeStruct((b, HG_HEADS, HEAD_DIM, HEAD_DIM), F32)],
        scratch_shapes=[pltpu.VMEM((HG_PAR, HEAD_DIM, HEAD_DIM), F32)],
        compiler_params=_cparams(("parallel", "parallel", "arbitrary")),
        name="hgrn2",
    )(zh, zh, zh, zh, lb_logits, out_norm.reshape(1, HG_WIDTH), s0)


def _sel_coverage(ns_pad, ns, ncp, nc):
    j = jnp.arange(ns_pad)[:, None]
    i = jnp.arange(ncp)[None, :]
    d = i - (SEL_RATIO * j - 1)
    w = jnp.zeros((ns_pad, ncp), F32)
    for dd, ww in enumerate(SEL_SPAN_W):
        w = jnp.where(d == dd, ww, w)
    return jnp.where((j < ns) & (i < nc), w, 0.0).astype(BF16)


def _pack_w_in(w):
    gates = w[:, ZA_COLS:ZA_COLS + 3 * NSA_HEADS]
    return jnp.concatenate([w[:, :ZA_COLS], w[:, ZA_COLS + 3 * NSA_HEADS:],
                            jnp.pad(gates, ((0, 0), (0, PROJ_TN - 3 * NSA_HEADS)))], axis=1).astype(BF16)


def _pack_ffn(wg, wu, wd, tf=512):
    f = wg.shape[1]
    fp = -(-f // tf) * tf
    padc = ((0, 0), (0, fp - f))
    return (jnp.pad(wg.astype(BF16), padc), jnp.pad(wu.astype(BF16), padc),
            jnp.pad(wd.astype(BF16), ((0, fp - f), (0, 0))))


def kernel(x_prompt, x_sample, cache_cmp_kv, cache_slc_kv, state_win_kv, state_hgrn, page_table, ffn1_norm, ffn1_w_gate, ffn1_w_up, ffn1_w_down, mix_norm, w_in, cmp_pe, cmp_w1, cmp_b1, cmp_w2, nsa_out_norm, hg_lb_logits, hg_out_norm, w_out, ffn2_norm, ffn2_w_gate, ffn2_w_up, ffn2_w_down, final_norm):
    depth = w_in.shape[0]
    assert depth == 1, "single-layer trunk"
    l = 0
    b, t, d = x_prompt.shape
    db, ts, _ = x_sample.shape
    n_pool = cache_cmp_kv.shape[1]
    n_pages = page_table.shape[1]
    past_len = n_pages * PAGE_SIZE
    assert state_win_kv.shape[2] == min(WINDOW, past_len) and t % SEL_BLOCK == 0

    f1 = _pack_ffn(ffn1_w_gate[l], ffn1_w_up[l], ffn1_w_down[l])
    f2 = _pack_ffn(ffn2_w_gate[l], ffn2_w_up[l], ffn2_w_down[l])
    w_in_p = _pack_w_in(w_in[l])
    w_out_p = w_out[l].astype(BF16)
    w1 = cmp_w1[l]
    w1r = (w1.reshape(2, CMP_BLOCK // CMP_STRIDE, CMP_STRIDE, HEAD_DIM, HEAD_DIM)
           .transpose(0, 2, 3, 1, 4).reshape(2, CMP_STRIDE, HEAD_DIM, 2 * HEAD_DIM).astype(BF16))
    w2 = cmp_w2[l].astype(BF16)
    cmp_bias = _cmp_bias(cmp_pe[l], w1, cmp_b1[l])
    gain = nsa_out_norm[l]

    def trunk_front(x):
        x1, h = _ffn(x, ffn1_norm[l], *f1, mix_norm[l], emit_x=True, norm_dtype=BF16)
        return (x1,) + tuple(_inproj(h, w_in_p))

    def trunk_back(x1, o_nsa, o_hg):
        x2 = _outproj(x1, o_nsa, o_hg, w_out_p)
        return _ffn(x2, ffn2_norm[l], *f2, final_norm, emit_x=False, norm_dtype=F32)[0]

    def kv_rows(kv, lead):
        return kv.reshape(1, *lead, 2, NSA_KV, HEAD_DIM)

    x1, za, za16, zh, zg, kvc, kvs, kvw = trunk_front(x_prompt.reshape(b * t, d))
    ns = t // SEL_BLOCK
    ncp = t // CMP_STRIDE
    msel = _sel_coverage(128, ns, ncp, ncp - 1)
    ckv = _cmp_prompt(za, b, t, w1r, cmp_bias, w2)
    o_nsa = _nsa_prompt(za16, ckv, zg, msel, gain, b, t)
    s0 = jnp.zeros((b, HG_HEADS, HEAD_DIM, HEAD_DIM), F32)
    o_hg, hg_p = _hgrn(zh, hg_lb_logits, hg_out_norm[l], s0, b, t, l, BF16)
    y_prompt = trunk_back(x1, o_nsa, o_hg).reshape(b, t, d)
    p_cmp = kv_rows(kvc, (b, t))
    p_slc = kv_rows(kvs, (b, t))
    p_win = kv_rows(kvw, (b, t))[:, :, t - min(WINDOW, t):]

    x1s, zas, _, zhs, zgs, kvc_s, kvs_s, kvw_s = trunk_front(x_sample.reshape(db * ts, d))
    ncp_s = past_len // CMP_STRIDE
    ns_s = -(-(past_len + ts) // SEL_BLOCK)
    ns_pad = -(-ns_s // 8) * 8
    msel_s = _sel_coverage(ns_pad, ns_s, ncp_s, ncp_s - 1)
    step_keys = SLC_PAGES * PAGE_SIZE
    e_blocks = (jnp.arange(step_keys)[None, :] // SEL_BLOCK == jnp.arange(step_keys // SEL_BLOCK)[:, None]).astype(BF16)
    ckv_s = _cmp_paged(cache_cmp_kv.reshape(-1, HEAD_DIM), l * n_pool, page_table, w1r, cmp_bias, w2)
    wlen = state_win_kv.shape[2]
    win = state_win_kv[l].reshape(db * wlen * KV_GROUPS, HEAD_DIM)
    o_nsa_s = _nsa_sample(zas, zgs, ckv_s, cache_slc_kv.reshape(-1, HEAD_DIM), l * n_pool, win,
                          page_table, msel_s, e_blocks, gain, ts)
    o_hg_s, hg_s = _hgrn(zhs, hg_lb_logits, hg_out_norm[l], state_hgrn[l], db, ts, l, F32)
    y_sample = trunk_back(x1s, o_nsa_s, o_hg_s).reshape(db, ts, d)
    s_cmp = kv_rows(kvc_s, (db, ts))
    s_slc = kv_rows(kvs_s, (db, ts))
    s_win = jnp.concatenate([state_win_kv[l:l + 1, :, ts:].astype(F32), kv_rows(kvw_s, (db, ts))], axis=2)

    return (y_prompt, y_sample, p_cmp, p_slc, p_win, hg_p[None].astype(state_hgrn.dtype),
            s_cmp, s_slc, s_win, hg_s[None].astype(state_hgrn.dtype))
```
